```python
import jax, jax.numpy as jnp
from jax import lax
import numpy as np

D_MODEL = 2048
BATCH = 4
SEQ = 4096
DEPTH = 2

N_HEADS = 16
N_KV_HEADS = 4
HEAD_DIM = 128
ATTN_WIDTH = N_HEADS * HEAD_DIM
KV_WIDTH = N_KV_HEADS * HEAD_DIM
WINDOW = 128
BLOCK = 128
LRU_WIDTH = D_MODEL
LRU_BLOCKS = 16
LRU_BLOCK_W = LRU_WIDTH // LRU_BLOCKS
LRU_CONV = 4
LRU_C = 8.0
CONV_WIDTH = D_MODEL
CONV_KERNEL = 31
N_BRANCH = 3
MEM_LEN = 256
XATTN_HEADS = 4
XATTN_HEAD_DIM = 128
XATTN_WIDTH = XATTN_HEADS * XATTN_HEAD_DIM
FF_HIDDEN = -(-(8 * D_MODEL) // (3 * 256)) * 256
EPS = 1e-6
NEG_INF = -1e30

N_IN = ATTN_WIDTH + 2 * KV_WIDTH + 2 * LRU_WIDTH + 2 * CONV_WIDTH + N_BRANCH * D_MODEL
SPLITS = list(np.cumsum([ATTN_WIDTH, KV_WIDTH, KV_WIDTH, LRU_WIDTH, LRU_WIDTH, 2 * CONV_WIDTH]))

kernel_name = "hybrid_gqa_rglru_conformer_encoder"


def rmsnorm(x, g):
    x32 = x.astype(jnp.float32)
    y = x32 * lax.rsqrt(jnp.mean(x32 * x32, axis=-1, keepdims=True) + EPS)
    return (y * g.astype(jnp.float32)).astype(x.dtype)


def layernorm(x, g, b):
    x32 = x.astype(jnp.float32)
    mu = jnp.mean(x32, axis=-1, keepdims=True)
    var = jnp.mean(jnp.square(x32 - mu), axis=-1, keepdims=True)
    y = (x32 - mu) * lax.rsqrt(var + EPS)
    return (y * g.astype(jnp.float32) + b.astype(jnp.float32)).astype(x.dtype)


def depthwise_conv(x, w, b, pad):
    C = x.shape[-1]
    y = lax.conv_general_dilated(x, w[:, None, :].astype(x.dtype), window_strides=(1,), padding=[pad],
                                 dimension_numbers=('NWC', 'WIO', 'NWC'), feature_group_count=C)
    return y + b


def alibi_slopes(n_heads):
    return jnp.exp2(-(8.0 / n_heads) * jnp.arange(1, n_heads + 1, dtype=jnp.float32))


def windowed_gqa(q, k, v, sink):
    B, S = q.shape[0], q.shape[1]
    nb = S // BLOCK
    grp = N_HEADS // N_KV_HEADS
    qb = q.reshape(B, nb, BLOCK, N_KV_HEADS, grp, HEAD_DIM)

    def bands(t):
        tp = jnp.pad(t, ((0, 0), (BLOCK, BLOCK), (0, 0), (0, 0)))
        tp = tp.reshape(B, nb + 2, BLOCK, N_KV_HEADS, HEAD_DIM)
        return jnp.concatenate([tp[:, :-2], tp[:, 1:-1], tp[:, 2:]], axis=2)

    kb, vb = bands(k), bands(v)
    s = jnp.einsum('bnqkgd,bnskd->bnkgqs', qb, kb,
                   preferred_element_type=jnp.float32) * (HEAD_DIM ** -0.5)
    qi = jnp.arange(BLOCK)
    kj = jnp.arange(3 * BLOCK)
    rel = qi[:, None] + BLOCK - kj[None, :]
    kpos = jnp.arange(nb)[:, None] * BLOCK - BLOCK + kj[None, :]
    valid = (jnp.abs(rel) <= WINDOW)[None] & ((kpos >= 0) & (kpos < S))[:, None, :]
    slopes = alibi_slopes(N_HEADS).reshape(N_KV_HEADS, grp)
    bias = -slopes[:, :, None, None] * jnp.abs(rel).astype(jnp.float32)[None, None]
    s = jnp.where(valid[None, :, None, None], s + bias, NEG_INF)
    sink_col = jnp.broadcast_to(sink.astype(jnp.float32).reshape(1, 1, N_KV_HEADS, grp, 1, 1),
                                s.shape[:-1] + (1,))
    p = jax.nn.softmax(jnp.concatenate([s, sink_col], axis=-1), axis=-1)[..., :-1]
    o = jnp.einsum('bnkgqs,bnskd->bnqkgd', p.astype(v.dtype), vb)
    return o.reshape(B, S, ATTN_WIDTH)


def _lin_combine(left, right):
    a1, b1 = left
    a2, b2 = right
    return a1 * a2, a2 * b1 + b2


def rglru_direction(xl, conv_w, conv_b, wr, br, wi, bi, lam, reverse):
    B, S, C = xl.shape
    pad = (0, LRU_CONV - 1) if reverse else (LRU_CONV - 1, 0)
    xc = depthwise_conv(xl, conv_w, conv_b, pad)
    xb = xc.reshape(B, S, LRU_BLOCKS, LRU_BLOCK_W)
    r = jax.nn.sigmoid(jnp.einsum('bsnc,ncd->bsnd', xb, wr).reshape(B, S, C) + br)
    i = jax.nn.sigmoid(jnp.einsum('bsnc,ncd->bsnd', xb, wi).reshape(B, S, C) + bi)
    log_a = -LRU_C * r.astype(jnp.float32) * jax.nn.softplus(-lam.astype(jnp.float32))
    a = jnp.exp(log_a)
    b = jnp.sqrt(-jnp.expm1(2.0 * log_a)) * (i * xc).astype(jnp.float32)
    _, h = lax.associative_scan(_lin_combine, (a, b), reverse=reverse, axis=1)
    return h.astype(xl.dtype)


def conformer_conv(u, dw_w, dw_b, ln_g, ln_b):
    u1, u2 = jnp.split(u, 2, axis=-1)
    glu = u1 * jax.nn.sigmoid(u2)
    half = (CONV_KERNEL - 1) // 2
    c = depthwise_conv(glu, dw_w, dw_b, (half, half))
    return jax.nn.silu(layernorm(c, ln_g, ln_b))


def memory_cross_attention(h, mem_n, wq, wkv, wo):
    B, S, _ = h.shape
    M = mem_n.shape[1]
    q = (h @ wq).reshape(B, S, XATTN_HEADS, XATTN_HEAD_DIM)
    k, v = jnp.split(mem_n @ wkv, 2, axis=-1)
    k = k.reshape(B, M, XATTN_HEADS, XATTN_HEAD_DIM)
    v = v.reshape(B, M, XATTN_HEADS, XATTN_HEAD_DIM)
    s = jnp.einsum('bshd,bmhd->bhsm', q, k, preferred_element_type=jnp.float32) * (XATTN_HEAD_DIM ** -0.5)
    p = jax.nn.softmax(s, axis=-1)
    o = jnp.einsum('bhsm,bmhd->bshd', p.astype(v.dtype), v).reshape(B, S, XATTN_WIDTH)
    return o @ wo


def setup_inputs(seed: int = 0) -> dict:
    key = jax.random.key(seed)
    ks = iter(jax.random.split(key, 40))
    f32 = jnp.float32

    def nrm(shape, scale):
        return jax.random.normal(next(ks), shape, f32) * scale

    def gain(shape):
        return 1.0 + nrm(shape, 0.02)

    u = jax.random.uniform(next(ks), (DEPTH, 2, LRU_WIDTH), f32, minval=0.9, maxval=0.999)
    s_base = u ** (1.0 / LRU_C)
    lru_lambda = jnp.log(s_base) - jnp.log1p(-s_base)
    return {
        "x": nrm((BATCH, SEQ, D_MODEL), 1.0),
        "mem": nrm((BATCH, MEM_LEN, D_MODEL), 1.0),
        "norm_mix": gain((DEPTH, D_MODEL)),
        "w_in": nrm((DEPTH, D_MODEL, N_IN), D_MODEL ** -0.5),
        "gate_bias": nrm((DEPTH, N_BRANCH * D_MODEL), 0.01),
        "attn_sink": nrm((DEPTH, N_HEADS), 0.5),
        "lru_conv_w": nrm((DEPTH, 2, LRU_CONV, LRU_WIDTH), LRU_CONV ** -0.5),
        "lru_conv_b": nrm((DEPTH, 2, LRU_WIDTH), 0.01),
        "lru_wr": nrm((DEPTH, 2, LRU_BLOCKS, LRU_BLOCK_W, LRU_BLOCK_W), LRU_BLOCK_W ** -0.5),
        "lru_br": nrm((DEPTH, 2, LRU_WIDTH), 0.01),
        "lru_wi": nrm((DEPTH, 2, LRU_BLOCKS, LRU_BLOCK_W, LRU_BLOCK_W), LRU_BLOCK_W ** -0.5),
        "lru_bi": nrm((DEPTH, 2, LRU_WIDTH), 0.01),
        "lru_lambda": lru_lambda,
        "conv_dw_w": nrm((DEPTH, CONV_KERNEL, CONV_WIDTH), CONV_KERNEL ** -0.5),
        "conv_dw_b": nrm((DEPTH, CONV_WIDTH), 0.01),
        "conv_ln_g": gain((DEPTH, CONV_WIDTH)),
        "conv_ln_b": nrm((DEPTH, CONV_WIDTH), 0.01),
        "w_proj_attn": nrm((DEPTH, ATTN_WIDTH, D_MODEL), ATTN_WIDTH ** -0.5),
        "w_proj_lru": nrm((DEPTH, LRU_WIDTH, D_MODEL), LRU_WIDTH ** -0.5),
        "w_proj_conv": nrm((DEPTH, CONV_WIDTH, D_MODEL), CONV_WIDTH ** -0.5),
        "w_out": nrm((DEPTH, D_MODEL, D_MODEL), D_MODEL ** -0.5),
        "norm_cross": gain((DEPTH, D_MODEL)),
        "norm_mem": gain((DEPTH, D_MODEL)),
        "xattn_wq": nrm((DEPTH, D_MODEL, XATTN_WIDTH), D_MODEL ** -0.5),
        "xattn_wkv": nrm((DEPTH, D_MODEL, 2 * XATTN_WIDTH), D_MODEL ** -0.5),
        "xattn_wo": nrm((DEPTH, XATTN_WIDTH, D_MODEL), XATTN_WIDTH ** -0.5),
        "norm_ffn": gain((DEPTH, D_MODEL)),
        "ffn_w13": nrm((DEPTH, D_MODEL, 2 * FF_HIDDEN), D_MODEL ** -0.5),
        "ffn_w2": nrm((DEPTH, FF_HIDDEN, D_MODEL), FF_HIDDEN ** -0.5),
        "norm_final": gain((D_MODEL,)),
    }


def reference(x, mem, norm_mix, w_in, gate_bias, attn_sink, lru_conv_w, lru_conv_b, lru_wr, lru_br,
              lru_wi, lru_bi, lru_lambda, conv_dw_w, conv_dw_b, conv_ln_g, conv_ln_b, w_proj_attn,
              w_proj_lru, w_proj_conv, w_out, norm_cross, norm_mem, xattn_wq, xattn_wkv, xattn_wo,
              norm_ffn, ffn_w13, ffn_w2, norm_final):
    B, S, _ = x.shape
    for l in range(DEPTH):
        h = rmsnorm(x, norm_mix[l])
        z = h @ w_in[l]
        q, k, v, xl, gl, u, gates = jnp.split(z, SPLITS, axis=-1)
        ya = windowed_gqa(q.reshape(B, S, N_HEADS, HEAD_DIM),
                          k.reshape(B, S, N_KV_HEADS, HEAD_DIM),
                          v.reshape(B, S, N_KV_HEADS, HEAD_DIM), attn_sink[l])
        h_fwd = rglru_direction(xl, lru_conv_w[l, 0], lru_conv_b[l, 0], lru_wr[l, 0], lru_br[l, 0],
                                lru_wi[l, 0], lru_bi[l, 0], lru_lambda[l, 0], reverse=False)
        h_bwd = rglru_direction(xl, lru_conv_w[l, 1], lru_conv_b[l, 1], lru_wr[l, 1], lru_br[l, 1],
                                lru_wi[l, 1], lru_bi[l, 1], lru_lambda[l, 1], reverse=True)
        yl = (h_fwd + h_bwd) * jax.nn.gelu(gl)
        yc = conformer_conv(u, conv_dw_w[l], conv_dw_b[l], conv_ln_g[l], conv_ln_b[l])
        g = jax.nn.sigmoid(gates + gate_bias[l]).reshape(B, S, N_BRANCH, D_MODEL)
        merged = (g[:, :, 0] * (ya @ w_proj_attn[l])
                  + g[:, :, 1] * (yl @ w_proj_lru[l])
                  + g[:, :, 2] * (yc @ w_proj_conv[l]))
        x = x + merged @ w_out[l]
        hc = rmsnorm(x, norm_cross[l])
        mem_n = rmsnorm(mem, norm_mem[l])
        x = x + memory_cross_attention(hc, mem_n, xattn_wq[l], xattn_wkv[l], xattn_wo[l])
        hf = rmsnorm(x, norm_ffn[l])
        a1, a3 = jnp.split(hf @ ffn_w13[l], 2, axis=-1)
        x = x + (jax.nn.silu(a1) * a3) @ ffn_w2[l]
    return rmsnorm(x, norm_final)
```

```python
import functools

import jax
import jax.numpy as jnp
from jax import lax
from jax.experimental import pallas as pl
from jax.experimental.pallas import tpu as pltpu

N_HEADS = 16
N_KV_HEADS = 4
HEAD_DIM = 128
WINDOW = 128
LRU_BLOCK_W = 128
LRU_CONV = 4
LRU_C = 8.0
XATTN_HEADS = 4
N_BRANCH = 3
EPS = 1e-6
NEG_INF = -1e30

SUBLANES = 8
VMEM_LIMIT_BYTES = 56 * 1024 * 1024

BF16 = jnp.bfloat16
F32 = jnp.float32


def _params(n_grid_axes):
    return pltpu.CompilerParams(
        dimension_semantics=("arbitrary",) * n_grid_axes,
        vmem_limit_bytes=VMEM_LIMIT_BYTES,
    )


def _dot(a, b):
    return jnp.dot(a, b, preferred_element_type=F32)


def _dot_t(a, b):
    return lax.dot_general(a, b, (((1,), (1,)), ((), ())), preferred_element_type=F32)


def _rmsnorm_rows(x, g):
    y = x * lax.rsqrt(jnp.mean(x * x, axis=-1, keepdims=True) + EPS)
    return y * g


def _rmsnorm_into(x_ref, g_ref, out_ref, chunk=256):
    rows = x_ref.shape[0]
    chunk = min(chunk, rows)
    for r0 in range(0, rows, chunk):
        out_ref[r0:r0 + chunk, :] = _rmsnorm_rows(x_ref[r0:r0 + chunk, :], g_ref[...]).astype(out_ref.dtype)


def _proj_kernel(x_ref, g_ref, w_ref, o_ref, hn_ref):
    @pl.when(pl.program_id(1) == 0)
    def _():
        _rmsnorm_into(x_ref, g_ref, hn_ref)

    o_ref[...] = _dot(hn_ref[...], w_ref[...]).astype(o_ref.dtype)


def _norm_proj(x, g, w, col_off, n_cols, out_dtype, tm, tn):
    T, K = x.shape
    tm = min(tm, T)
    assert T % tm == 0 and n_cols % tn == 0 and col_off % tn == 0
    off = col_off // tn
    return pl.pallas_call(
        _proj_kernel,
        grid=(T // tm, n_cols // tn),
        in_specs=[
            pl.BlockSpec((tm, K), lambda i, j: (i, 0)),
            pl.BlockSpec((1, K), lambda i, j: (0, 0)),
            pl.BlockSpec((K, tn), lambda i, j: (0, j + off)),
        ],
        out_specs=pl.BlockSpec((tm, tn), lambda i, j: (i, j)),
        out_shape=jax.ShapeDtypeStruct((T, n_cols), out_dtype),
        scratch_shapes=[pltpu.VMEM((tm, K), BF16)],
        compiler_params=_params(2),
        name="norm_proj",
    )(x, g.reshape(1, K), w)


def _attn_kernel(slopes_ref, sink_ref, q_ref, k_ref, v_ref, o_ref, *, tq, seq):
    kh = pl.program_id(1)
    qs = pl.program_id(2) * tq
    win = tq + 2 * WINDOW
    ws = pl.multiple_of(jnp.clip(qs - WINDOW, 0, seq - win), WINDOW)
    kb = k_ref[pl.ds(ws, win), :]
    vb = v_ref[pl.ds(ws, win), :]
    tpos = qs + lax.broadcasted_iota(jnp.int32, (tq, win), 0)
    kpos = ws + lax.broadcasted_iota(jnp.int32, (tq, win), 1)
    dist = jnp.abs(tpos - kpos)
    valid = dist <= WINDOW
    distf = dist.astype(F32)
    scale = HEAD_DIM ** -0.5
    grp = N_HEADS // N_KV_HEADS
    for g in range(grp):
        h = kh * grp + g
        cols = slice(g * HEAD_DIM, (g + 1) * HEAD_DIM)
        s = _dot_t(q_ref[:, cols], kb) * scale
        s = jnp.where(valid, s - slopes_ref[h] * distf, NEG_INF)
        sink = sink_ref[h]
        m = jnp.maximum(jnp.max(s, axis=-1, keepdims=True), sink)
        p = jnp.exp(s - m)
        denom = jnp.sum(p, axis=-1, keepdims=True) + jnp.exp(sink - m)
        o = _dot(p.astype(BF16), vb) / denom
        o_ref[:, cols] = o.astype(o_ref.dtype)


def _window_attention(qkv, slopes, sink, batch, seq, tq):
    tq = min(tq, seq - 2 * WINDOW)
    assert seq % tq == 0 and tq % WINDOW == 0
    grp_w = (N_HEADS // N_KV_HEADS) * HEAD_DIM
    k_blk = N_HEADS
    v_blk = N_HEADS + N_KV_HEADS
    smem = pl.BlockSpec(memory_space=pltpu.SMEM)
    return pl.pallas_call(
        functools.partial(_attn_kernel, tq=tq, seq=seq),
        grid=(batch, N_KV_HEADS, seq // tq),
        in_specs=[
            smem,
            smem,
            pl.BlockSpec((None, tq, grp_w), lambda b, h, i: (b, i, h)),
            pl.BlockSpec((None, seq, HEAD_DIM), lambda b, h, i: (b, 0, k_blk + h)),
            pl.BlockSpec((None, seq, HEAD_DIM), lambda b, h, i: (b, 0, v_blk + h)),
        ],
        out_specs=pl.BlockSpec((None, tq, grp_w), lambda b, h, i: (b, i, h)),
        out_shape=jax.ShapeDtypeStruct((batch, seq, N_HEADS * HEAD_DIM), BF16),
        compiler_params=_params(3),
        name="window_attention",
    )(slopes, sink, qkv, qkv, qkv)


def _softplus(x):
    return jnp.maximum(x, 0.0) + jnp.log1p(jnp.exp(-jnp.abs(x)))


def _lru_kernel(xl_ref, gl_ref, cw_ref, cb_ref, wr_ref, br_ref, wi_ref, bi_ref, lam_ref, o_ref,
                xp_ref, hf_ref, *, seq, cb, rc):
    pad = SUBLANES
    nblk = cb // LRU_BLOCK_W
    xp_ref[0:pad, :] = jnp.zeros((pad, cb), F32)
    xp_ref[pad + seq:pad + seq + pad, :] = jnp.zeros((pad, cb), F32)
    xp_ref[pad:pad + seq, :] = xl_ref[...]
    row = lax.broadcasted_iota(jnp.int32, (SUBLANES, cb), 0)
    n_chunks = seq // rc
    n_win = rc + pad

    def gates(d, t0):
        if d == 0:
            wstart = pl.multiple_of(t0, SUBLANES)
            win = xp_ref[pl.ds(wstart, n_win), :]
            taps = [pltpu.roll(win, LRU_CONV - 1 - k, 0)[pad:pad + rc] if k < LRU_CONV - 1
                    else win[pad:pad + rc] for k in range(LRU_CONV)]
        else:
            wstart = pl.multiple_of(t0 + pad, SUBLANES)
            win = xp_ref[pl.ds(wstart, n_win), :]
            taps = [pltpu.roll(win, n_win - k, 0)[0:rc] if k > 0 else win[0:rc]
                    for k in range(LRU_CONV)]
        xc = taps[0] * cw_ref[d, 0:1, :]
        for k in range(1, LRU_CONV):
            xc = xc + taps[k] * cw_ref[d, k:k + 1, :]
        xc = xc + cb_ref[d:d + 1, :]
        xcb = xc.astype(BF16)
        rs, is_ = [], []
        for n in range(nblk):
            cols = slice(n * LRU_BLOCK_W, (n + 1) * LRU_BLOCK_W)
            rs.append(_dot(xcb[:, cols], wr_ref[d, n]))
            is_.append(_dot(xcb[:, cols], wi_ref[d, n]))
        r = jax.nn.sigmoid(jnp.concatenate(rs, axis=-1) + br_ref[d:d + 1, :])
        ig = jax.nn.sigmoid(jnp.concatenate(is_, axis=-1) + bi_ref[d:d + 1, :])
        log_a = (-LRU_C * r) * _softplus(-lam_ref[d:d + 1, :])
        a = jnp.exp(log_a)
        gain = jnp.sqrt(-jnp.tanh(log_a) * (a * a + 1.0))
        return a, gain * (ig * xc)

    def tile_scan(a, b, reverse):
        for k in (1, 2, 4):
            if reverse:
                keep = row < SUBLANES - k
                sh = SUBLANES - k
            else:
                keep = row >= k
                sh = k
            a_sh = jnp.where(keep, pltpu.roll(a, sh, 0), 1.0)
            b_sh = jnp.where(keep, pltpu.roll(b, sh, 0), 0.0)
            b = a * b_sh + b
            a = a * a_sh
        return a, b

    def fwd_chunk(c, carry):
        t0 = c * rc
        a, b = gates(0, t0)
        hs = []
        for j in range(rc // SUBLANES):
            sl = slice(j * SUBLANES, (j + 1) * SUBLANES)
            at, bt = tile_scan(a[sl], b[sl], False)
            h = at * carry + bt
            carry = h[SUBLANES - 1:SUBLANES, :]
            hs.append(h)
        hf_ref[pl.ds(pl.multiple_of(t0, SUBLANES), rc), :] = jnp.concatenate(hs, axis=0)
        return carry

    lax.fori_loop(0, n_chunks, fwd_chunk, jnp.zeros((1, cb), F32))

    def bwd_chunk(c, carry):
        t0 = pl.multiple_of((n_chunks - 1 - c) * rc, SUBLANES)
        a, b = gates(1, t0)
        hs = [None] * (rc // SUBLANES)
        for j in reversed(range(rc // SUBLANES)):
            sl = slice(j * SUBLANES, (j + 1) * SUBLANES)
            at, bt = tile_scan(a[sl], b[sl], True)
            h = at * carry + bt
            carry = h[0:1, :]
            hs[j] = h
        hb = jnp.concatenate(hs, axis=0)
        y = (hf_ref[pl.ds(t0, rc), :] + hb) * jax.nn.gelu(gl_ref[pl.ds(t0, rc), :])
        o_ref[pl.ds(t0, rc), :] = y.astype(o_ref.dtype)
        return carry

    lax.fori_loop(0, n_chunks, bwd_chunk, jnp.zeros((1, cb), F32))


def _rglru(rest, conv_w, conv_b, wr, br, wi, bi, lam, batch, seq, cb, rc):
    C = conv_w.shape[-1]
    rc = min(rc, seq)
    assert C % cb == 0 and cb % LRU_BLOCK_W == 0 and seq % rc == 0
    ncb = C // cb
    nblk = cb // LRU_BLOCK_W
    vec = pl.BlockSpec((2, cb), lambda b, c: (0, c))
    wspec = pl.BlockSpec((2, nblk, LRU_BLOCK_W, LRU_BLOCK_W), lambda b, c: (0, c, 0, 0))
    return pl.pallas_call(
        functools.partial(_lru_kernel, seq=seq, cb=cb, rc=rc),
        grid=(batch, ncb),
        in_specs=[
            pl.BlockSpec((None, seq, cb), lambda b, c: (b, 0, c)),
            pl.BlockSpec((None, seq, cb), lambda b, c: (b, 0, ncb + c)),
            pl.BlockSpec((2, LRU_CONV, cb), lambda b, c: (0, 0, c)),
            vec, wspec, vec, wspec, vec, vec,
        ],
        out_specs=pl.BlockSpec((None, seq, cb), lambda b, c: (b, 0, c)),
        out_shape=jax.ShapeDtypeStruct((batch, seq, C), BF16),
        scratch_shapes=[pltpu.VMEM((seq + 2 * SUBLANES, cb), F32), pltpu.VMEM((seq, cb), F32)],
        compiler_params=_params(2),
        name="rglru",
    )(rest, rest, conv_w, conv_b, wr, br, wi, bi, lam)


def _conformer_kernel(u1_ref, u2_ref, u1p_ref, u2p_ref, u1n_ref, u2n_ref, w_ref, b_ref, g_ref, beta_ref,
                      o_ref, glu_ref, conv_ref, *, rows, halo, seq, ktaps, rr, cc):
    t0 = pl.program_id(0) * rows
    first = (t0 % seq) == 0
    last = ((t0 + rows) % seq) == 0
    C = o_ref.shape[-1]
    glu_ref[halo:halo + rows, :] = u1_ref[...] * jax.nn.sigmoid(u2_ref[...])
    prev = u1p_ref[...] * jax.nn.sigmoid(u2p_ref[...])
    glu_ref[0:halo, :] = jnp.where(first, 0.0, prev)
    nxt = u1n_ref[...] * jax.nn.sigmoid(u2n_ref[...])
    glu_ref[halo + rows:halo + rows + halo, :] = jnp.where(last, 0.0, nxt)
    half = (ktaps - 1) // 2
    base = halo - half
    for r0 in range(0, rows, rr):
        for c0 in range(0, C, cc):
            cs = slice(c0, c0 + cc)
            acc = glu_ref[base + r0:base + r0 + rr, cs] * w_ref[0:1, cs]
            for k in range(1, ktaps):
                acc = acc + glu_ref[base + r0 + k:base + r0 + k + rr, cs] * w_ref[k:k + 1, cs]
            conv_ref[r0:r0 + rr, cs] = acc + b_ref[:, cs]
    lr = min(64, rows)
    for r0 in range(0, rows, lr):
        c = conv_ref[r0:r0 + lr, :]
        mu = jnp.mean(c, axis=-1, keepdims=True)
        d = c - mu
        var = jnp.mean(d * d, axis=-1, keepdims=True)
        y = d * lax.rsqrt(var + EPS) * g_ref[...] + beta_ref[...]
        o_ref[r0:r0 + lr, :] = (y * jax.nn.sigmoid(y)).astype(o_ref.dtype)


def _conformer(rest, col_u1, dw_w, dw_b, ln_g, ln_b, seq, rows):
    T = rest.shape[0]
    ktaps, C = dw_w.shape
    halo = 2 * SUBLANES
    rows = min(rows, seq)
    assert (ktaps - 1) // 2 <= halo and seq % rows == 0 and rows % halo == 0
    hb = rows // halo
    n_halo_blocks = T // halo
    main = lambda col: pl.BlockSpec((rows, C), lambda i: (i, col))
    prev = lambda col: pl.BlockSpec((halo, C), lambda i: (jnp.maximum(i * hb - 1, 0), col))
    nxt = lambda col: pl.BlockSpec((halo, C), lambda i: (jnp.minimum((i + 1) * hb, n_halo_blocks - 1), col))
    vec = pl.BlockSpec((1, C), lambda i: (0, 0))
    return pl.pallas_call(
        functools.partial(_conformer_kernel, rows=rows, halo=halo, seq=seq, ktaps=ktaps, rr=32, cc=512),
        grid=(T // rows,),
        in_specs=[main(col_u1), main(col_u1 + 1), prev(col_u1), prev(col_u1 + 1), nxt(col_u1), nxt(col_u1 + 1),
                  pl.BlockSpec((ktaps, C), lambda i: (0, 0)), vec, vec, vec],
        out_specs=pl.BlockSpec((rows, C), lambda i: (i, 0)),
        out_shape=jax.ShapeDtypeStruct((T, C), BF16),
        scratch_shapes=[pltpu.VMEM((rows + 2 * halo, C), F32), pltpu.VMEM((rows, C), F32)],
        compiler_params=_params(1),
        name="conformer_conv",
    )(rest, rest, rest, rest, rest, rest, dw_w, dw_b.reshape(1, C), ln_g.reshape(1, C), ln_b.reshape(1, C))


def _merge_kernel(ya_ref, yl_ref, yc_ref, wa_ref, wl_ref, wc_ref, g0_ref, g1_ref, g2_ref,
                  b0_ref, b1_ref, b2_ref, o_ref):
    def branch(y_ref, w_ref, g_ref, b_ref):
        return jax.nn.sigmoid(g_ref[...] + b_ref[...]) * _dot(y_ref[...], w_ref[...])

    m = branch(ya_ref, wa_ref, g0_ref, b0_ref) + branch(yl_ref, wl_ref, g1_ref, b1_ref)
    m = m + branch(yc_ref, wc_ref, g2_ref, b2_ref)
    o_ref[...] = m.astype(o_ref.dtype)


def _merge(ya, yl, yc, wa, wl, wc, rest, gate_col, gate_bias, tm, tn):
    T, D = ya.shape
    tm = min(tm, T)
    assert T % tm == 0 and D % tn == 0 and gate_col % tn == 0
    nb = D // tn
    gc = gate_col // tn
    act = pl.BlockSpec((tm, D), lambda i, j: (i, 0))
    wsp = pl.BlockSpec((D, tn), lambda i, j: (0, j))
    gate = lambda n: pl.BlockSpec((tm, tn), lambda i, j: (i, gc + n * nb + j))
    bias = lambda n: pl.BlockSpec((1, tn), lambda i, j: (0, n * nb + j))
    return pl.pallas_call(
        _merge_kernel,
        grid=(T // tm, nb),
        in_specs=[act, act, act, wsp, wsp, wsp, gate(0), gate(1), gate(2), bias(0), bias(1), bias(2)],
        out_specs=pl.BlockSpec((tm, tn), lambda i, j: (i, j)),
        out_shape=jax.ShapeDtypeStruct((T, D), BF16),
        compiler_params=_params(2),
        name="gated_merge",
    )(ya, yl, yc, wa, wl, wc, rest, rest, rest, gate_bias, gate_bias, gate_bias)


def _resmm_kernel(h_ref, w_ref, x_ref, o_ref):
    o_ref[...] = x_ref[...] + _dot(h_ref[...], w_ref[...])


def _residual_matmul(h, w, x, tm, tn):
    T, K = h.shape
    N = w.shape[1]
    tm = min(tm, T)
    assert T % tm == 0 and N % tn == 0
    return pl.pallas_call(
        _resmm_kernel,
        grid=(T // tm, N // tn),
        in_specs=[
            pl.BlockSpec((tm, K), lambda i, j: (i, 0)),
            pl.BlockSpec((K, tn), lambda i, j: (0, j)),
            pl.BlockSpec((tm, tn), lambda i, j: (i, j)),
        ],
        out_specs=pl.BlockSpec((tm, tn), lambda i, j: (i, j)),
        out_shape=jax.ShapeDtypeStruct((T, N), F32),
        compiler_params=_params(2),
        name="residual_matmul",
    )(h, w, x)


def _xattn_kernel(x_ref, g_ref, wq_ref, kv_ref, wo_ref, o_ref, hn_ref):
    _rmsnorm_into(x_ref, g_ref, hn_ref)
    q = _dot(hn_ref[...], wq_ref[...]).astype(BF16)
    width = wq_ref.shape[1]
    hd = width // XATTN_HEADS
    scale = hd ** -0.5
    outs = []
    for h in range(XATTN_HEADS):
        cols = slice(h * hd, (h + 1) * hd)
        s = _dot_t(q[:, cols], kv_ref[:, cols]) * scale
        m = jnp.max(s, axis=-1, keepdims=True)
        p = jnp.exp(s - m)
        denom = jnp.sum(p, axis=-1, keepdims=True)
        vcols = slice(width + h * hd, width + (h + 1) * hd)
        outs.append((_dot(p.astype(BF16), kv_ref[:, vcols]) / denom).astype(BF16))
    o = jnp.concatenate(outs, axis=-1)
    o_ref[...] = x_ref[...] + _dot(o, wo_ref[...])


def _cross_attention(x, g, wq, kv, wo, seq, tm):
    T, D = x.shape
    width = wq.shape[1]
    mem_len = kv.shape[1]
    tm = min(tm, seq)
    assert seq % tm == 0
    per_seq = seq // tm
    return pl.pallas_call(
        _xattn_kernel,
        grid=(T // tm,),
        in_specs=[
            pl.BlockSpec((tm, D), lambda i: (i, 0)),
            pl.BlockSpec((1, D), lambda i: (0, 0)),
            pl.BlockSpec((D, width), lambda i: (0, 0)),
            pl.BlockSpec((None, mem_len, 2 * width), lambda i: (i // per_seq, 0, 0)),
            pl.BlockSpec((width, D), lambda i: (0, 0)),
        ],
        out_specs=pl.BlockSpec((tm, D), lambda i: (i, 0)),
        out_shape=jax.ShapeDtypeStruct((T, D), F32),
        scratch_shapes=[pltpu.VMEM((tm, D), BF16)],
        compiler_params=_params(1),
        name="memory_cross_attention",
    )(x, g.reshape(1, D), wq, kv, wo)


def _ffn_up_kernel(x_ref, g_ref, w1_ref, w3_ref, o_ref, hn_ref):
    @pl.when(pl.program_id(1) == 0)
    def _():
        _rmsnorm_into(x_ref, g_ref, hn_ref)

    hn = hn_ref[...]
    a1 = _dot(hn, w1_ref[...])
    a3 = _dot(hn, w3_ref[...])
    o_ref[...] = (a1 * jax.nn.sigmoid(a1) * a3).astype(o_ref.dtype)


def _ffn_up(x, g, w13, tm, tn):
    T, D = x.shape
    hidden = w13.shape[1] // 2
    tm = min(tm, T)
    assert T % tm == 0 and hidden % tn == 0
    nb = hidden // tn
    return pl.pallas_call(
        _ffn_up_kernel,
        grid=(T // tm, nb),
        in_specs=[
            pl.BlockSpec((tm, D), lambda i, j: (i, 0)),
            pl.BlockSpec((1, D), lambda i, j: (0, 0)),
            pl.BlockSpec((D, tn), lambda i, j: (0, j)),
            pl.BlockSpec((D, tn), lambda i, j: (0, nb + j)),
        ],
        out_specs=pl.BlockSpec((tm, tn), lambda i, j: (i, j)),
        out_shape=jax.ShapeDtypeStruct((T, hidden), BF16),
        scratch_shapes=[pltpu.VMEM((tm, D), BF16)],
        compiler_params=_params(2),
        name="ffn_up",
    )(x, g.reshape(1, D), w13, w13)


def _rmsnorm_kernel(x_ref, g_ref, o_ref):
    _rmsnorm_into(x_ref, g_ref, o_ref)


def _rmsnorm(x, g, tm):
    T, D = x.shape
    tm = min(tm, T)
    return pl.pallas_call(
        _rmsnorm_kernel,
        grid=(T // tm,),
        in_specs=[pl.BlockSpec((tm, D), lambda i: (i, 0)), pl.BlockSpec((1, D), lambda i: (0, 0))],
        out_specs=pl.BlockSpec((tm, D), lambda i: (i, 0)),
        out_shape=jax.ShapeDtypeStruct((T, D), F32),
        compiler_params=_params(1),
        name="final_rmsnorm",
    )(x, g.reshape(1, D))


def kernel(x, mem, norm_mix, w_in, gate_bias, attn_sink, lru_conv_w, lru_conv_b, lru_wr, lru_br, lru_wi, lru_bi, lru_lambda, conv_dw_w, conv_dw_b, conv_ln_g, conv_ln_b, w_proj_attn, w_proj_lru, w_proj_conv, w_out, norm_cross, norm_mem, xattn_wq, xattn_wkv, xattn_wo, norm_ffn, ffn_w13, ffn_w2, norm_final):
    B, S, D = x.shape
    M = mem.shape[1]
    T = B * S
    depth = w_in.shape[0]
    attn_w = N_HEADS * HEAD_DIM
    qkv_w = attn_w + 2 * N_KV_HEADS * HEAD_DIM
    lru_w = lru_conv_w.shape[-1]
    conv_c = conv_dw_w.shape[-1]
    rest_w = w_in.shape[2] - qkv_w
    assert lru_w == D and conv_c == D
    u_col = 2 * lru_w
    gate_col = u_col + 2 * conv_c
    slopes = jnp.exp2(-(8.0 / N_HEADS) * jnp.arange(1, N_HEADS + 1, dtype=F32))

    xf = x.reshape(T, D)
    memf = mem.reshape(B * M, D)
    bf = lambda a: a.astype(BF16)
    for l in range(depth):
        w_in_l = bf(w_in[l])
        qkv = _norm_proj(xf, norm_mix[l], w_in_l, 0, qkv_w, BF16, tm=1024, tn=512)
        rest = _norm_proj(xf, norm_mix[l], w_in_l, qkv_w, rest_w, F32, tm=1024, tn=512)
        ya = _window_attention(qkv.reshape(B, S, qkv_w), slopes, attn_sink[l], B, S, tq=256)
        yl = _rglru(rest.reshape(B, S, rest_w), lru_conv_w[l], lru_conv_b[l], bf(lru_wr[l]), lru_br[l],
                    bf(lru_wi[l]), lru_bi[l], lru_lambda[l], B, S, cb=256, rc=128)
        yc = _conformer(rest, u_col // conv_c, conv_dw_w[l], conv_dw_b[l], conv_ln_g[l], conv_ln_b[l], S, rows=256)
        merged = _merge(ya.reshape(T, attn_w), yl.reshape(T, lru_w), yc, bf(w_proj_attn[l]), bf(w_proj_lru[l]),
                        bf(w_proj_conv[l]), rest, gate_col, gate_bias[l].reshape(1, N_BRANCH * D), tm=512, tn=512)
        xf = _residual_matmul(merged, bf(w_out[l]), xf, tm=1024, tn=512)
        kv = _norm_proj(memf, norm_mem[l], bf(xattn_wkv[l]), 0, xattn_wkv.shape[2], BF16, tm=1024, tn=512)
        xf = _cross_attention(xf, norm_cross[l], bf(xattn_wq[l]), kv.reshape(B, M, -1), bf(xattn_wo[l]), S, tm=512)
        hidden = _ffn_up(xf, norm_ffn[l], bf(ffn_w13[l]), tm=1024, tn=512)
        xf = _residual_matmul(hidden, bf(ffn_w2[l]), xf, tm=512, tn=512)
    return _rmsnorm(xf, norm_final, tm=512).reshape(B, S, D)
```

```python
import functools

import jax
import jax.numpy as jnp
from jax import lax
from jax.experimental import pallas as pl
from jax.experimental.pallas import tpu as pltpu

N_HEADS = 16
N_KV_HEADS = 4
HEAD_DIM = 128
WINDOW = 128
LRU_BLOCK_W = 128
LRU_CONV = 4
LRU_C = 8.0
XATTN_HEADS = 4
N_BRANCH = 3
EPS = 1e-6
NEG_INF = -1e30

SUBLANES = 8
VMEM_LIMIT_BYTES = 56 * 1024 * 1024

BF16 = jnp.bfloat16
F32 = jnp.float32


def _params(n_grid_axes):
    return pltpu.CompilerParams(
        dimension_semantics=("arbitrary",) * n_grid_axes,
        vmem_limit_bytes=VMEM_LIMIT_BYTES,
    )


def _dot(a, b):
    return jnp.dot(a, b, preferred_element_type=F32)


def _dot_t(a, b):
    return lax.dot_general(a, b, (((1,), (1,)), ((), ())), preferred_element_type=F32)


def _rmsnorm_rows(x, g):
    y = x * lax.rsqrt(jnp.mean(x * x, axis=-1, keepdims=True) + EPS)
    return y * g


def _rmsnorm_into(x_ref, g_ref, out_ref, chunk=256):
    rows = x_ref.shape[0]
    chunk = min(chunk, rows)
    for r0 in range(0, rows, chunk):
        out_ref[r0:r0 + chunk, :] = _rmsnorm_rows(x_ref[r0:r0 + chunk, :], g_ref[...]).astype(out_ref.dtype)


def _proj_kernel(x_ref, g_ref, w_ref, o_ref, hn_ref):
    @pl.when(pl.program_id(1) == 0)
    def _():
        _rmsnorm_into(x_ref, g_ref, hn_ref)

    o_ref[...] = _dot(hn_ref[...], w_ref[...].astype(BF16)).astype(o_ref.dtype)


def _weight_spec(w, layer, tn, col_block_of):
    if w.ndim == 2:
        return pl.BlockSpec((w.shape[0], tn), lambda i, j: (0, col_block_of(j)))
    return pl.BlockSpec((None, w.shape[1], tn), lambda i, j: (layer, 0, col_block_of(j)))


def _norm_proj(x, g, w, layer, col_off, n_cols, out_dtype, tm, tn):
    T, K = x.shape
    tm = min(tm, T)
    assert T % tm == 0 and n_cols % tn == 0 and col_off % tn == 0
    off = col_off // tn
    return pl.pallas_call(
        _proj_kernel,
        grid=(T // tm, n_cols // tn),
        in_specs=[
            pl.BlockSpec((tm, K), lambda i, j: (i, 0)),
            pl.BlockSpec((1, K), lambda i, j: (0, 0)),
            _weight_spec(w, layer, tn, lambda j: j + off),
        ],
        out_specs=pl.BlockSpec((tm, tn), lambda i, j: (i, j)),
        out_shape=jax.ShapeDtypeStruct((T, n_cols), out_dtype),
        scratch_shapes=[pltpu.VMEM((tm, K), BF16)],
        compiler_params=_params(2),
        name="norm_proj",
    )(x, g.reshape(1, K), w)


def _attn_kernel(slopes_ref, sink_ref, q_ref, k_ref, v_ref, o_ref, *, tq, seq):
    kh = pl.program_id(1)
    win = 3 * WINDOW
    scale = HEAD_DIM ** -0.5
    grp = N_HEADS // N_KV_HEADS
    rel = (lax.broadcasted_iota(jnp.int32, (WINDOW, win), 0)
           - lax.broadcasted_iota(jnp.int32, (WINDOW, win), 1))
    for j in range(tq // WINDOW):
        rows = slice(j * WINDOW, (j + 1) * WINDOW)
        qs = pl.program_id(2) * tq + j * WINDOW
        ws = pl.multiple_of(jnp.clip(qs - WINDOW, 0, seq - win), WINDOW)
        kb = k_ref[pl.ds(ws, win), :]
        vb = v_ref[pl.ds(ws, win), :]
        dist = jnp.abs(rel + (qs - ws))
        valid = dist <= WINDOW
        distf = dist.astype(F32)
        q_stack = jnp.concatenate(
            [q_ref[rows, g * HEAD_DIM:(g + 1) * HEAD_DIM] for g in range(grp)], axis=0)
        s_all = _dot_t(q_stack, kb) * scale
        probs, inv_den = [], []
        for g in range(grp):
            h = kh * grp + g
            s = s_all[g * WINDOW:(g + 1) * WINDOW]
            s = jnp.where(valid, s - slopes_ref[h] * distf, NEG_INF)
            sink = sink_ref[h]
            m = jnp.maximum(jnp.max(s, axis=-1, keepdims=True), sink)
            p = jnp.exp(s - m)
            inv_den.append(1.0 / (jnp.sum(p, axis=-1, keepdims=True) + jnp.exp(sink - m)))
            probs.append(p.astype(BF16))
        o_all = _dot(jnp.concatenate(probs, axis=0), vb)
        for g in range(grp):
            o = o_all[g * WINDOW:(g + 1) * WINDOW] * inv_den[g]
            o_ref[rows, g * HEAD_DIM:(g + 1) * HEAD_DIM] = o.astype(o_ref.dtype)


def _window_attention(qkv, slopes, sink, batch, seq, tq):
    tq = min(tq, seq)
    assert seq % tq == 0 and tq % WINDOW == 0 and seq >= 3 * WINDOW
    grp_w = (N_HEADS // N_KV_HEADS) * HEAD_DIM
    k_blk = N_HEADS
    v_blk = N_HEADS + N_KV_HEADS
    smem = pl.BlockSpec(memory_space=pltpu.SMEM)
    return pl.pallas_call(
        functools.partial(_attn_kernel, tq=tq, seq=seq),
        grid=(batch, N_KV_HEADS, seq // tq),
        in_specs=[
            smem,
            smem,
            pl.BlockSpec((None, tq, grp_w), lambda b, h, i: (b, i, h)),
            pl.BlockSpec((None, seq, HEAD_DIM), lambda b, h, i: (b, 0, k_blk + h)),
            pl.BlockSpec((None, seq, HEAD_DIM), lambda b, h, i: (b, 0, v_blk + h)),
        ],
        out_specs=pl.BlockSpec((None, tq, grp_w), lambda b, h, i: (b, i, h)),
        out_shape=jax.ShapeDtypeStruct((batch, seq, N_HEADS * HEAD_DIM), BF16),
        compiler_params=_params(3),
        name="window_attention",
    )(slopes, sink, qkv, qkv, qkv)


def _softplus(x):
    return jnp.maximum(x, 0.0) + jnp.log1p(jnp.exp(-jnp.abs(x)))


def _lru_kernel(xl_ref, gl_ref, cw_ref, cb_ref, wr_ref, br_ref, wi_ref, bi_ref, lam_ref, o_ref,
                xp_ref, hf_ref, *, seq, cb, rc):
    pad = SUBLANES
    nblk = cb // LRU_BLOCK_W
    xp_ref[0:pad, :] = jnp.zeros((pad, cb), F32)
    xp_ref[pad + seq:pad + seq + pad, :] = jnp.zeros((pad, cb), F32)
    xp_ref[pad:pad + seq, :] = xl_ref[...]
    row = lax.broadcasted_iota(jnp.int32, (SUBLANES, cb), 0)
    n_chunks = seq // rc
    n_win = rc + pad

    def gates(d, t0):
        if d == 0:
            wstart = pl.multiple_of(t0, SUBLANES)
            win = xp_ref[pl.ds(wstart, n_win), :]
            taps = [pltpu.roll(win, LRU_CONV - 1 - k, 0)[pad:pad + rc] if k < LRU_CONV - 1
                    else win[pad:pad + rc] for k in range(LRU_CONV)]
        else:
            wstart = pl.multiple_of(t0 + pad, SUBLANES)
            win = xp_ref[pl.ds(wstart, n_win), :]
            taps = [pltpu.roll(win, n_win - k, 0)[0:rc] if k > 0 else win[0:rc]
                    for k in range(LRU_CONV)]
        xc = taps[0] * cw_ref[d, 0:1, :]
        for k in range(1, LRU_CONV):
            xc = xc + taps[k] * cw_ref[d, k:k + 1, :]
        xc = xc + cb_ref[d:d + 1, :]
        xcb = xc.astype(BF16)
        rs, is_ = [], []
        for n in range(nblk):
            cols = slice(n * LRU_BLOCK_W, (n + 1) * LRU_BLOCK_W)
            rs.append(_dot(xcb[:, cols], wr_ref[d, n]))
            is_.append(_dot(xcb[:, cols], wi_ref[d, n]))
        r = jax.nn.sigmoid(jnp.concatenate(rs, axis=-1) + br_ref[d:d + 1, :])
        ig = jax.nn.sigmoid(jnp.concatenate(is_, axis=-1) + bi_ref[d:d + 1, :])
        log_a = (-LRU_C * r) * _softplus(-lam_ref[d:d + 1, :])
        a = jnp.exp(log_a)
        gain = jnp.sqrt(-jnp.tanh(log_a) * (a * a + 1.0))
        return a, gain * (ig * xc)

    def tile_scan(a, b, reverse):
        for k in (1, 2, 4):
            if reverse:
                keep = row < SUBLANES - k
                sh = SUBLANES - k
            else:
                keep = row >= k
                sh = k
            a_sh = jnp.where(keep, pltpu.roll(a, sh, 0), 1.0)
            b_sh = jnp.where(keep, pltpu.roll(b, sh, 0), 0.0)
            b = a * b_sh + b
            a = a * a_sh
        return a, b

    def fwd_chunk(c, carry):
        t0 = c * rc
        a, b = gates(0, t0)
        hs = []
        for j in range(rc // SUBLANES):
            sl = slice(j * SUBLANES, (j + 1) * SUBLANES)
            at, bt = tile_scan(a[sl], b[sl], False)
            h = at * carry + bt
            carry = h[SUBLANES - 1:SUBLANES, :]
            hs.append(h)
        hf_ref[pl.ds(pl.multiple_of(t0, SUBLANES), rc), :] = jnp.concatenate(hs, axis=0)
        return carry

    lax.fori_loop(0, n_chunks, fwd_chunk, jnp.zeros((1, cb), F32))

    def bwd_chunk(c, carry):
        t0 = pl.multiple_of((n_chunks - 1 - c) * rc, SUBLANES)
        a, b = gates(1, t0)
        hs = [None] * (rc // SUBLANES)
        for j in reversed(range(rc // SUBLANES)):
            sl = slice(j * SUBLANES, (j + 1) * SUBLANES)
            at, bt = tile_scan(a[sl], b[sl], True)
            h = at * carry + bt
            carry = h[0:1, :]
            hs[j] = h
        hb = jnp.concatenate(hs, axis=0)
        y = (hf_ref[pl.ds(t0, rc), :] + hb) * jax.nn.gelu(gl_ref[pl.ds(t0, rc), :])
        o_ref[pl.ds(t0, rc), :] = y.astype(o_ref.dtype)
        return carry

    lax.fori_loop(0, n_chunks, bwd_chunk, jnp.zeros((1, cb), F32))


def _rglru(rest, conv_w, conv_b, wr, br, wi, bi, lam, batch, seq, cb, rc):
    C = conv_w.shape[-1]
    rc = min(rc, seq)
    assert C % cb == 0 and cb % LRU_BLOCK_W == 0 and seq % rc == 0
    ncb = C // cb
    nblk = cb // LRU_BLOCK_W
    vec = pl.BlockSpec((2, cb), lambda b, c: (0, c))
    wspec = pl.BlockSpec((2, nblk, LRU_BLOCK_W, LRU_BLOCK_W), lambda b, c: (0, c, 0, 0))
    return pl.pallas_call(
        functools.partial(_lru_kernel, seq=seq, cb=cb, rc=rc),
        grid=(batch, ncb),
        in_specs=[
            pl.BlockSpec((None, seq, cb), lambda b, c: (b, 0, c)),
            pl.BlockSpec((None, seq, cb), lambda b, c: (b, 0, ncb + c)),
            pl.BlockSpec((2, LRU_CONV, cb), lambda b, c: (0, 0, c)),
            vec, wspec, vec, wspec, vec, vec,
        ],
        out_specs=pl.BlockSpec((None, seq, cb), lambda b, c: (b, 0, c)),
        out_shape=jax.ShapeDtypeStruct((batch, seq, C), BF16),
        scratch_shapes=[pltpu.VMEM((seq + 2 * SUBLANES, cb), F32), pltpu.VMEM((seq, cb), F32)],
        compiler_params=_params(2),
        name="rglru",
    )(rest, rest, conv_w, conv_b, wr, br, wi, bi, lam)


def _conformer_kernel(u1_ref, u2_ref, u1p_ref, u2p_ref, u1n_ref, u2n_ref, w_ref, b_ref, g_ref, beta_ref,
                      o_ref, glu_ref, conv_ref, *, rows, halo, seq, ktaps, rr, cc):
    t0 = pl.program_id(0) * rows
    first = (t0 % seq) == 0
    last = ((t0 + rows) % seq) == 0
    C = o_ref.shape[-1]
    glu_ref[0, halo:halo + rows, :] = u1_ref[...] * jax.nn.sigmoid(u2_ref[...])
    prev = u1p_ref[...] * jax.nn.sigmoid(u2p_ref[...])
    glu_ref[0, 0:halo, :] = jnp.where(first, 0.0, prev)
    nxt = u1n_ref[...] * jax.nn.sigmoid(u2n_ref[...])
    glu_ref[0, halo + rows:halo + rows + halo, :] = jnp.where(last, 0.0, nxt)
    half = (ktaps - 1) // 2
    base = halo - half
    n_shift_rows = rows + 2 * halo - SUBLANES
    sr = 56
    assert n_shift_rows % sr == 0
    for s in range(1, SUBLANES):
        for r0 in range(0, n_shift_rows, sr):
            for c0 in range(0, C, cc):
                cs = slice(c0, c0 + cc)
                blk = glu_ref[0, r0:r0 + sr + SUBLANES, cs]
                glu_ref[s, r0:r0 + sr, cs] = pltpu.roll(blk, sr + SUBLANES - s, 0)[0:sr]
    for r0 in range(0, rows, rr):
        for c0 in range(0, C, cc):
            cs = slice(c0, c0 + cc)
            n_out = rr // SUBLANES
            accs = [None] * n_out
            for s in range(SUBLANES):
                taps = [k for k in range(ktaps) if (base + k) % SUBLANES == s]
                if not taps:
                    continue
                t_lo = min((base + k) // SUBLANES for k in taps)
                t_hi = max((base + k) // SUBLANES for k in taps) + n_out
                tiles = {i: glu_ref[s, r0 + i * SUBLANES:r0 + (i + 1) * SUBLANES, cs]
                         for i in range(t_lo, t_hi)}
                for k in taps:
                    wk = w_ref[k, :, cs]
                    a = (base + k) // SUBLANES
                    for t in range(n_out):
                        term = tiles[a + t] * wk
                        accs[t] = term if accs[t] is None else accs[t] + term
            for t in range(n_out):
                lo = r0 + t * SUBLANES
                conv_ref[lo:lo + SUBLANES, cs] = accs[t] + b_ref[:, cs]
    lr = min(64, rows)
    for r0 in range(0, rows, lr):
        c = conv_ref[r0:r0 + lr, :]
        mu = jnp.mean(c, axis=-1, keepdims=True)
        d = c - mu
        var = jnp.mean(d * d, axis=-1, keepdims=True)
        y = d * lax.rsqrt(var + EPS) * g_ref[...] + beta_ref[...]
        o_ref[r0:r0 + lr, :] = (y * jax.nn.sigmoid(y)).astype(o_ref.dtype)


def _conformer(rest, col_u1, dw_w, dw_b, ln_g, ln_b, seq, rows):
    T = rest.shape[0]
    ktaps, C = dw_w.shape
    halo = 2 * SUBLANES
    rows = min(rows, seq)
    assert (ktaps - 1) // 2 <= halo and seq % rows == 0 and rows % halo == 0
    hb = rows // halo
    n_halo_blocks = T // halo
    main = lambda col: pl.BlockSpec((rows, C), lambda i: (i, col))
    prev = lambda col: pl.BlockSpec((halo, C), lambda i: (jnp.maximum(i * hb - 1, 0), col))
    nxt = lambda col: pl.BlockSpec((halo, C), lambda i: (jnp.minimum((i + 1) * hb, n_halo_blocks - 1), col))
    vec = pl.BlockSpec((1, C), lambda i: (0, 0))
    return pl.pallas_call(
        functools.partial(_conformer_kernel, rows=rows, halo=halo, seq=seq, ktaps=ktaps, rr=64, cc=256),
        grid=(T // rows,),
        in_specs=[main(col_u1), main(col_u1 + 1), prev(col_u1), prev(col_u1 + 1), nxt(col_u1), nxt(col_u1 + 1),
                  pl.BlockSpec((ktaps, SUBLANES, C), lambda i: (0, 0, 0)), vec, vec, vec],
        out_specs=pl.BlockSpec((rows, C), lambda i: (i, 0)),
        out_shape=jax.ShapeDtypeStruct((T, C), BF16),
        scratch_shapes=[pltpu.VMEM((SUBLANES, rows + 2 * halo, C), F32), pltpu.VMEM((rows, C), F32)],
        compiler_params=_params(1),
        name="conformer_conv",
    )(rest, rest, rest, rest, rest, rest, jnp.broadcast_to(dw_w[:, None, :], (ktaps, SUBLANES, C)),
      dw_b.reshape(1, C), ln_g.reshape(1, C), ln_b.reshape(1, C))


def _merge_kernel(ya_ref, yl_ref, yc_ref, wa_ref, wl_ref, wc_ref, g0_ref, g1_ref, g2_ref,
                  b0_ref, b1_ref, b2_ref, o_ref):
    def branch(y_ref, w_ref, g_ref, b_ref):
        return jax.nn.sigmoid(g_ref[...] + b_ref[...]) * _dot(y_ref[...], w_ref[...])

    m = branch(ya_ref, wa_ref, g0_ref, b0_ref) + branch(yl_ref, wl_ref, g1_ref, b1_ref)
    m = m + branch(yc_ref, wc_ref, g2_ref, b2_ref)
    o_ref[...] = m.astype(o_ref.dtype)


def _merge(ya, yl, yc, wa, wl, wc, rest, gate_col, gate_bias, tm, tn):
    T, D = ya.shape
    tm = min(tm, T)
    assert T % tm == 0 and D % tn == 0 and gate_col % tn == 0
    nb = D // tn
    gc = gate_col // tn
    act = pl.BlockSpec((tm, D), lambda i, j: (i, 0))
    wsp = pl.BlockSpec((D, tn), lambda i, j: (0, j))
    gate = lambda n: pl.BlockSpec((tm, tn), lambda i, j: (i, gc + n * nb + j))
    bias = lambda n: pl.BlockSpec((1, tn), lambda i, j: (0, n * nb + j))
    return pl.pallas_call(
        _merge_kernel,
        grid=(T // tm, nb),
        in_specs=[act, act, act, wsp, wsp, wsp, gate(0), gate(1), gate(2), bias(0), bias(1), bias(2)],
        out_specs=pl.BlockSpec((tm, tn), lambda i, j: (i, j)),
        out_shape=jax.ShapeDtypeStruct((T, D), BF16),
        compiler_params=_params(2),
        name="gated_merge",
    )(ya, yl, yc, wa, wl, wc, rest, rest, rest, gate_bias, gate_bias, gate_bias)


def _resmm_kernel(h_ref, w_ref, x_ref, o_ref):
    o_ref[...] = x_ref[...] + _dot(h_ref[...], w_ref[...].astype(BF16))


def _residual_matmul(h, w, layer, x, tm, tn):
    T, K = h.shape
    N = w.shape[-1]
    tm = min(tm, T)
    assert T % tm == 0 and N % tn == 0
    return pl.pallas_call(
        _resmm_kernel,
        grid=(T // tm, N // tn),
        in_specs=[
            pl.BlockSpec((tm, K), lambda i, j: (i, 0)),
            _weight_spec(w, layer, tn, lambda j: j),
            pl.BlockSpec((tm, tn), lambda i, j: (i, j)),
        ],
        out_specs=pl.BlockSpec((tm, tn), lambda i, j: (i, j)),
        out_shape=jax.ShapeDtypeStruct((T, N), F32),
        compiler_params=_params(2),
        name="residual_matmul",
    )(h, w, x)


def _xattn_kernel(x_ref, g_ref, wq_ref, kv_ref, wo_ref, o_ref, hn_ref):
    _rmsnorm_into(x_ref, g_ref, hn_ref)
    q = _dot(hn_ref[...], wq_ref[...]).astype(BF16)
    width = wq_ref.shape[1]
    hd = width // XATTN_HEADS
    scale = hd ** -0.5
    outs = []
    for h in range(XATTN_HEADS):
        cols = slice(h * hd, (h + 1) * hd)
        s = _dot_t(q[:, cols], kv_ref[:, cols]) * scale
        m = jnp.max(s, axis=-1, keepdims=True)
        p = jnp.exp(s - m)
        denom = jnp.sum(p, axis=-1, keepdims=True)
        vcols = slice(width + h * hd, width + (h + 1) * hd)
        outs.append((_dot(p.astype(BF16), kv_ref[:, vcols]) / denom).astype(BF16))
    o = jnp.concatenate(outs, axis=-1)
    o_ref[...] = x_ref[...] + _dot(o, wo_ref[...])


def _cross_attention(x, g, wq, kv, wo, seq, tm):
    T, D = x.shape
    width = wq.shape[1]
    mem_len = kv.shape[1]
    tm = min(tm, seq)
    assert seq % tm == 0
    per_seq = seq // tm
    return pl.pallas_call(
        _xattn_kernel,
        grid=(T // tm,),
        in_specs=[
            pl.BlockSpec((tm, D), lambda i: (i, 0)),
            pl.BlockSpec((1, D), lambda i: (0, 0)),
            pl.BlockSpec((D, width), lambda i: (0, 0)),
            pl.BlockSpec((None, mem_len, 2 * width), lambda i: (i // per_seq, 0, 0)),
            pl.BlockSpec((width, D), lambda i: (0, 0)),
        ],
        out_specs=pl.BlockSpec((tm, D), lambda i: (i, 0)),
        out_shape=jax.ShapeDtypeStruct((T, D), F32),
        scratch_shapes=[pltpu.VMEM((tm, D), BF16)],
        compiler_params=_params(1),
        name="memory_cross_attention",
    )(x, g.reshape(1, D), wq, kv, wo)


def _ffn_up_kernel(x_ref, g_ref, w1_ref, w3_ref, o_ref, hn_ref):
    @pl.when(pl.program_id(1) == 0)
    def _():
        _rmsnorm_into(x_ref, g_ref, hn_ref)

    hn = hn_ref[...]
    a1 = _dot(hn, w1_ref[...].astype(BF16))
    a3 = _dot(hn, w3_ref[...].astype(BF16))
    o_ref[...] = (a1 * jax.nn.sigmoid(a1) * a3).astype(o_ref.dtype)


def _ffn_up(x, g, w13, layer, tm, tn):
    T, D = x.shape
    hidden = w13.shape[-1] // 2
    tm = min(tm, T)
    assert T % tm == 0 and hidden % tn == 0
    nb = hidden // tn
    return pl.pallas_call(
        _ffn_up_kernel,
        grid=(T // tm, nb),
        in_specs=[
            pl.BlockSpec((tm, D), lambda i, j: (i, 0)),
            pl.BlockSpec((1, D), lambda i, j: (0, 0)),
            _weight_spec(w13, layer, tn, lambda j: j),
            _weight_spec(w13, layer, tn, lambda j: nb + j),
        ],
        out_specs=pl.BlockSpec((tm, tn), lambda i, j: (i, j)),
        out_shape=jax.ShapeDtypeStruct((T, hidden), BF16),
        scratch_shapes=[pltpu.VMEM((tm, D), BF16)],
        compiler_params=_params(2),
        name="ffn_up",
    )(x, g.reshape(1, D), w13, w13)


def _rmsnorm_kernel(x_ref, g_ref, o_ref):
    _rmsnorm_into(x_ref, g_ref, o_ref)


def _rmsnorm(x, g, tm):
    T, D = x.shape
    tm = min(tm, T)
    return pl.pallas_call(
        _rmsnorm_kernel,
        grid=(T // tm,),
        in_specs=[pl.BlockSpec((tm, D), lambda i: (i, 0)), pl.BlockSpec((1, D), lambda i: (0, 0))],
        out_specs=pl.BlockSpec((tm, D), lambda i: (i, 0)),
        out_shape=jax.ShapeDtypeStruct((T, D), F32),
        compiler_params=_params(1),
        name="final_rmsnorm",
    )(x, g.reshape(1, D))


def kernel(x, mem, norm_mix, w_in, gate_bias, attn_sink, lru_conv_w, lru_conv_b, lru_wr, lru_br, lru_wi, lru_bi, lru_lambda, conv_dw_w, conv_dw_b, conv_ln_g, conv_ln_b, w_proj_attn, w_proj_lru, w_proj_conv, w_out, norm_cross, norm_mem, xattn_wq, xattn_wkv, xattn_wo, norm_ffn, ffn_w13, ffn_w2, norm_final):
    B, S, D = x.shape
    M = mem.shape[1]
    T = B * S
    depth = w_in.shape[0]
    attn_w = N_HEADS * HEAD_DIM
    qkv_w = attn_w + 2 * N_KV_HEADS * HEAD_DIM
    lru_w = lru_conv_w.shape[-1]
    conv_c = conv_dw_w.shape[-1]
    rest_w = w_in.shape[2] - qkv_w
    assert lru_w == D and conv_c == D
    u_col = 2 * lru_w
    gate_col = u_col + 2 * conv_c
    slopes = jnp.exp2(-(8.0 / N_HEADS) * jnp.arange(1, N_HEADS + 1, dtype=F32))

    xf = x.reshape(T, D)
    memf = mem.reshape(B * M, D)
    bf = lambda a: a.astype(BF16)
    for l in range(depth):
        qkv = _norm_proj(xf, norm_mix[l], w_in, l, 0, qkv_w, BF16, tm=1024, tn=512)
        rest = _norm_proj(xf, norm_mix[l], w_in, l, qkv_w, rest_w, F32, tm=1024, tn=512)
        ya = _window_attention(qkv.reshape(B, S, qkv_w), slopes, attn_sink[l], B, S, tq=512)
        yl = _rglru(rest.reshape(B, S, rest_w), lru_conv_w[l], lru_conv_b[l], bf(lru_wr[l]), lru_br[l],
                    bf(lru_wi[l]), lru_bi[l], lru_lambda[l], B, S, cb=256, rc=128)
        yc = _conformer(rest, u_col // conv_c, conv_dw_w[l], conv_dw_b[l], conv_ln_g[l], conv_ln_b[l], S, rows=256)
        merged = _merge(ya.reshape(T, attn_w), yl.reshape(T, lru_w), yc, bf(w_proj_attn[l]), bf(w_proj_lru[l]),
                        bf(w_proj_conv[l]), rest, gate_col, gate_bias[l].reshape(1, N_BRANCH * D), tm=512, tn=512)
        xf = _residual_matmul(merged, bf(w_out[l]), None, xf, tm=1024, tn=512)
        kv = _norm_proj(memf, norm_mem[l], xattn_wkv, l, 0, xattn_wkv.shape[2], BF16, tm=1024, tn=512)
        xf = _cross_attention(xf, norm_cross[l], bf(xattn_wq[l]), kv.reshape(B, M, -1), bf(xattn_wo[l]), S, tm=512)
        hidden = _ffn_up(xf, norm_ffn[l], ffn_w13, l, tm=1024, tn=512)
        xf = _residual_matmul(hidden, ffn_w2, l, xf, tm=1024, tn=256)
    return _rmsnorm(xf, norm_final, tm=512).reshape(B, S, D)
```

```python
import functools

import jax
import jax.numpy as jnp
from jax import lax
from jax.experimental import pallas as pl
from jax.experimental.pallas import tpu as pltpu

N_HEADS = 16
N_KV_HEADS = 4
HEAD_DIM = 128
WINDOW = 128
LRU_BLOCK_W = 128
LRU_CONV = 4
LRU_C = 8.0
XATTN_HEADS = 4
N_BRANCH = 3
EPS = 1e-6
NEG_INF = -1e30

SUBLANES = 8
LRU_SEG = 4
VMEM_LIMIT_BYTES = 56 * 1024 * 1024

BF16 = jnp.bfloat16
F32 = jnp.float32


def _params(n_grid_axes):
    return pltpu.CompilerParams(
        dimension_semantics=("arbitrary",) * n_grid_axes,
        vmem_limit_bytes=VMEM_LIMIT_BYTES,
    )


def _dot(a, b):
    return jnp.dot(a, b, preferred_element_type=F32)


def _dot_t(a, b):
    return lax.dot_general(a, b, (((1,), (1,)), ((), ())), preferred_element_type=F32)


def _rmsnorm_rows(x, g):
    y = x * lax.rsqrt(jnp.mean(x * x, axis=-1, keepdims=True) + EPS)
    return y * g


def _rmsnorm_into(x_ref, g_ref, out_ref, chunk=256):
    rows = x_ref.shape[0]
    chunk = min(chunk, rows)
    for r0 in range(0, rows, chunk):
        out_ref[r0:r0 + chunk, :] = _rmsnorm_rows(x_ref[r0:r0 + chunk, :], g_ref[...]).astype(out_ref.dtype)


def _proj_kernel(x_ref, g_ref, w_ref, o_ref, hn_ref):
    @pl.when(pl.program_id(1) == 0)
    def _():
        _rmsnorm_into(x_ref, g_ref, hn_ref)

    o_ref[...] = _dot(hn_ref[...], w_ref[...].astype(BF16)).astype(o_ref.dtype)


def _weight_spec(w, layer, tn, col_block_of):
    if w.ndim == 2:
        return pl.BlockSpec((w.shape[0], tn), lambda i, j: (0, col_block_of(j)))
    return pl.BlockSpec((None, w.shape[1], tn), lambda i, j: (layer, 0, col_block_of(j)))


def _norm_proj(x, g, w, layer, col_off, n_cols, out_dtype, tm, tn):
    T, K = x.shape
    tm = min(tm, T)
    assert T % tm == 0 and n_cols % tn == 0 and col_off % tn == 0
    off = col_off // tn
    return pl.pallas_call(
        _proj_kernel,
        grid=(T // tm, n_cols // tn),
        in_specs=[
            pl.BlockSpec((tm, K), lambda i, j: (i, 0)),
            pl.BlockSpec((1, K), lambda i, j: (0, 0)),
            _weight_spec(w, layer, tn, lambda j: j + off),
        ],
        out_specs=pl.BlockSpec((tm, tn), lambda i, j: (i, j)),
        out_shape=jax.ShapeDtypeStruct((T, n_cols), out_dtype),
        scratch_shapes=[pltpu.VMEM((tm, K), BF16)],
        compiler_params=_params(2),
        name="norm_proj",
    )(x, g.reshape(1, K), w)


def _attn_kernel(slopes_ref, sink_ref, q_ref, k_ref, v_ref, o_ref, *, tq, seq):
    kh = pl.program_id(1)
    win = 3 * WINDOW
    scale = HEAD_DIM ** -0.5
    grp = N_HEADS // N_KV_HEADS
    rel = (lax.broadcasted_iota(jnp.int32, (WINDOW, win), 0)
           - lax.broadcasted_iota(jnp.int32, (WINDOW, win), 1))
    for j in range(tq // WINDOW):
        rows = slice(j * WINDOW, (j + 1) * WINDOW)
        qs = pl.program_id(2) * tq + j * WINDOW
        ws = pl.multiple_of(jnp.clip(qs - WINDOW, 0, seq - win), WINDOW)
        kb = k_ref[pl.ds(ws, win), :]
        vb = v_ref[pl.ds(ws, win), :]
        dist = jnp.abs(rel + (qs - ws))
        valid = dist <= WINDOW
        distf = dist.astype(F32)
        q_stack = jnp.concatenate(
            [q_ref[rows, g * HEAD_DIM:(g + 1) * HEAD_DIM] for g in range(grp)], axis=0)
        s_all = _dot_t(q_stack, kb) * scale
        probs, inv_den = [], []
        for g in range(grp):
            h = kh * grp + g
            s = s_all[g * WINDOW:(g + 1) * WINDOW]
            s = jnp.where(valid, s - slopes_ref[h] * distf, NEG_INF)
            sink = sink_ref[h]
            m = jnp.maximum(jnp.max(s, axis=-1, keepdims=True), sink)
            p = jnp.exp(s - m)
            inv_den.append(1.0 / (jnp.sum(p, axis=-1, keepdims=True) + jnp.exp(sink - m)))
            probs.append(p.astype(BF16))
        o_all = _dot(jnp.concatenate(probs, axis=0), vb)
        for g in range(grp):
            o = o_all[g * WINDOW:(g + 1) * WINDOW] * inv_den[g]
            o_ref[rows, g * HEAD_DIM:(g + 1) * HEAD_DIM] = o.astype(o_ref.dtype)


def _window_attention(qkv, slopes, sink, batch, seq, tq):
    tq = min(tq, seq)
    assert seq % tq == 0 and tq % WINDOW == 0 and seq >= 3 * WINDOW
    grp_w = (N_HEADS // N_KV_HEADS) * HEAD_DIM
    k_blk = N_HEADS
    v_blk = N_HEADS + N_KV_HEADS
    smem = pl.BlockSpec(memory_space=pltpu.SMEM)
    return pl.pallas_call(
        functools.partial(_attn_kernel, tq=tq, seq=seq),
        grid=(batch, N_KV_HEADS, seq // tq),
        in_specs=[
            smem,
            smem,
            pl.BlockSpec((None, tq, grp_w), lambda b, h, i: (b, i, h)),
            pl.BlockSpec((None, seq, HEAD_DIM), lambda b, h, i: (b, 0, k_blk + h)),
            pl.BlockSpec((None, seq, HEAD_DIM), lambda b, h, i: (b, 0, v_blk + h)),
        ],
        out_specs=pl.BlockSpec((None, tq, grp_w), lambda b, h, i: (b, i, h)),
        out_shape=jax.ShapeDtypeStruct((batch, seq, N_HEADS * HEAD_DIM), BF16),
        compiler_params=_params(3),
        name="window_attention",
    )(slopes, sink, qkv, qkv, qkv)


def _softplus(x):
    return jnp.maximum(x, 0.0) + jnp.log1p(jnp.exp(-jnp.abs(x)))


def _lru_kernel(xl_ref, gl_ref, cw_ref, cb_ref, wr_ref, br_ref, wi_ref, bi_ref, lam_ref, o_ref,
                xp_ref, y_ref, *, seq, rc):
    pad = SUBLANES
    cb = LRU_BLOCK_W
    group = SUBLANES * LRU_SEG
    n_groups = rc // group
    n_chunks = seq // rc
    xp_ref[0:pad, :] = jnp.zeros((pad, cb), F32)
    xp_ref[pad + seq:pad + seq + pad, :] = jnp.zeros((pad, cb), F32)
    xp_ref[pad:pad + seq, :] = xl_ref[...]
    row = lax.broadcasted_iota(jnp.int32, (SUBLANES, cb), 0)

    def bcast(v):
        return jnp.broadcast_to(v, (SUBLANES, cb))

    def seg_tile(ref, start):
        return ref[pl.ds(start, SUBLANES, stride=LRU_SEG), :]

    def gates(d, t0):
        w = [bcast(cw_ref[d, k:k + 1, :]) for k in range(LRU_CONV)]
        bias = bcast(cb_ref[d:d + 1, :])
        lo = -(LRU_CONV - 1) if d == 0 else 0
        xcs = []
        for g in range(n_groups):
            base = t0 + g * group + pad
            x = {o: seg_tile(xp_ref, base + o) for o in range(lo, lo + LRU_SEG + LRU_CONV - 1)}
            for j in range(LRU_SEG):
                acc = x[lo + j] * w[0]
                for k in range(1, LRU_CONV):
                    acc = acc + x[lo + j + k] * w[k]
                xcs.append(acc + bias)
        xc = jnp.concatenate(xcs, axis=0)
        xcb = xc.astype(BF16)
        r = jax.nn.sigmoid(_dot(xcb, wr_ref[d, 0]) + br_ref[d:d + 1, :])
        ig = jax.nn.sigmoid(_dot(xcb, wi_ref[d, 0]) + bi_ref[d:d + 1, :])
        log_a = r * (-LRU_C * _softplus(-lam_ref[d:d + 1, :]))
        a = jnp.exp(log_a)
        gain = jnp.sqrt(-jnp.tanh(log_a) * (a * a + 1.0))
        return a, gain * (ig * xc)

    def tile_scan(a, b, reverse):
        for k in (1, 2, 4):
            if reverse:
                keep = row < SUBLANES - k
                sh = SUBLANES - k
            else:
                keep = row >= k
                sh = k
            a_sh = jnp.where(keep, pltpu.roll(a, sh, 0), 1.0)
            b_sh = jnp.where(keep, pltpu.roll(b, sh, 0), 0.0)
            b = a * b_sh + b
            a = a * a_sh
        return a, b

    def group_scan(a, b, g, carry, reverse):
        tile = lambda v, j: v[(g * LRU_SEG + j) * SUBLANES:(g * LRU_SEG + j + 1) * SUBLANES]
        order = list(reversed(range(LRU_SEG))) if reverse else list(range(LRU_SEG))
        hloc, prod = {}, {}
        prev = None
        for j in order:
            aj, bj = tile(a, j), tile(b, j)
            if prev is None:
                hloc[j], prod[j] = bj, aj
            else:
                hloc[j], prod[j] = aj * hloc[prev] + bj, aj * prod[prev]
            prev = j
        a_seg, b_seg = tile_scan(prod[prev], hloc[prev], reverse)
        end = a_seg * carry + b_seg
        if reverse:
            cin = jnp.where(row == SUBLANES - 1, carry, pltpu.roll(end, SUBLANES - 1, 0))
            carry = bcast(end[0:1, :])
        else:
            cin = jnp.where(row == 0, carry, pltpu.roll(end, 1, 0))
            carry = bcast(end[SUBLANES - 1:SUBLANES, :])
        return [hloc[j] + prod[j] * cin for j in range(LRU_SEG)], carry

    def fwd_chunk(c, carry):
        t0 = c * rc
        a, b = gates(0, t0)
        for g in range(n_groups):
            hs, carry = group_scan(a, b, g, carry, False)
            for j in range(LRU_SEG):
                y_ref[pl.ds(t0 + g * group + j, SUBLANES, stride=LRU_SEG), :] = hs[j]
        return carry

    unroll = min(8, n_chunks)
    assert n_chunks % unroll == 0
    lax.fori_loop(0, n_chunks, fwd_chunk, jnp.zeros((SUBLANES, cb), F32), unroll=unroll)

    def bwd_chunk(c, carry):
        t0 = (n_chunks - 1 - c) * rc
        a, b = gates(1, t0)
        for g in reversed(range(n_groups)):
            hs, carry = group_scan(a, b, g, carry, True)
            for j in range(LRU_SEG):
                start = t0 + g * group + j
                y = (seg_tile(y_ref, start) + hs[j]) * jax.nn.gelu(seg_tile(gl_ref, start))
                y_ref[pl.ds(start, SUBLANES, stride=LRU_SEG), :] = y
        return carry

    lax.fori_loop(0, n_chunks, bwd_chunk, jnp.zeros((SUBLANES, cb), F32), unroll=unroll)
    oc = min(256, seq)
    for r0 in range(0, seq, oc):
        o_ref[r0:r0 + oc, :] = y_ref[r0:r0 + oc, :].astype(o_ref.dtype)


def _rglru(rest, conv_w, conv_b, wr, br, wi, bi, lam, batch, seq, rc):
    C = conv_w.shape[-1]
    cb = LRU_BLOCK_W
    rc = min(rc, seq)
    assert C % cb == 0 and seq % rc == 0 and rc % (SUBLANES * LRU_SEG) == 0
    ncb = C // cb
    nblk = 1
    vec = pl.BlockSpec((2, cb), lambda b, c: (0, c))
    wspec = pl.BlockSpec((2, nblk, LRU_BLOCK_W, LRU_BLOCK_W), lambda b, c: (0, c, 0, 0))
    return pl.pallas_call(
        functools.partial(_lru_kernel, seq=seq, rc=rc),
        grid=(batch, ncb),
        in_specs=[
            pl.BlockSpec((None, seq, cb), lambda b, c: (b, 0, c)),
            pl.BlockSpec((None, seq, cb), lambda b, c: (b, 0, ncb + c)),
            pl.BlockSpec((2, LRU_CONV, cb), lambda b, c: (0, 0, c)),
            vec, wspec, vec, wspec, vec, vec,
        ],
        out_specs=pl.BlockSpec((None, seq, cb), lambda b, c: (b, 0, c)),
        out_shape=jax.ShapeDtypeStruct((batch, seq, C), BF16),
        scratch_shapes=[pltpu.VMEM((seq + 2 * SUBLANES, cb), F32), pltpu.VMEM((seq, cb), F32)],
        compiler_params=_params(2),
        name="rglru",
    )(rest, rest, conv_w, conv_b, wr, br, wi, bi, lam)


def _conformer_kernel(u1_ref, u2_ref, u1p_ref, u2p_ref, u1n_ref, u2n_ref, w_ref, b_ref, g_ref, beta_ref,
                      o_ref, glu_ref, conv_ref, *, rows, halo, seq, ktaps, rr, cc):
    t0 = pl.program_id(0) * rows
    first = (t0 % seq) == 0
    last = ((t0 + rows) % seq) == 0
    C = o_ref.shape[-1]
    glu_ref[0, halo:halo + rows, :] = u1_ref[...] * jax.nn.sigmoid(u2_ref[...])
    prev = u1p_ref[...] * jax.nn.sigmoid(u2p_ref[...])
    glu_ref[0, 0:halo, :] = jnp.where(first, 0.0, prev)
    nxt = u1n_ref[...] * jax.nn.sigmoid(u2n_ref[...])
    glu_ref[0, halo + rows:halo + rows + halo, :] = jnp.where(last, 0.0, nxt)
    half = (ktaps - 1) // 2
    base = halo - half
    n_shift_rows = rows + 2 * halo - SUBLANES
    sr = 56
    assert n_shift_rows % sr == 0
    for s in range(1, SUBLANES):
        for r0 in range(0, n_shift_rows, sr):
            for c0 in range(0, C, cc):
                cs = slice(c0, c0 + cc)
                blk = glu_ref[0, r0:r0 + sr + SUBLANES, cs]
                glu_ref[s, r0:r0 + sr, cs] = pltpu.roll(blk, sr + SUBLANES - s, 0)[0:sr]
    for r0 in range(0, rows, rr):
        for c0 in range(0, C, cc):
            cs = slice(c0, c0 + cc)
            n_out = rr // SUBLANES
            accs = [None] * n_out
            for s in range(SUBLANES):
                taps = [k for k in range(ktaps) if (base + k) % SUBLANES == s]
                if not taps:
                    continue
                t_lo = min((base + k) // SUBLANES for k in taps)
                t_hi = max((base + k) // SUBLANES for k in taps) + n_out
                tiles = {i: glu_ref[s, r0 + i * SUBLANES:r0 + (i + 1) * SUBLANES, cs]
                         for i in range(t_lo, t_hi)}
                for k in taps:
                    wk = w_ref[k, :, cs]
                    a = (base + k) // SUBLANES
                    for t in range(n_out):
                        term = tiles[a + t] * wk
                        accs[t] = term if accs[t] is None else accs[t] + term
            for t in range(n_out):
                lo = r0 + t * SUBLANES
                conv_ref[lo:lo + SUBLANES, cs] = accs[t] + b_ref[:, cs]
    lr = min(64, rows)
    for r0 in range(0, rows, lr):
        c = conv_ref[r0:r0 + lr, :]
        mu = jnp.mean(c, axis=-1, keepdims=True)
        d = c - mu
        var = jnp.mean(d * d, axis=-1, keepdims=True)
        y = d * lax.rsqrt(var + EPS) * g_ref[...] + beta_ref[...]
        o_ref[r0:r0 + lr, :] = (y * jax.nn.sigmoid(y)).astype(o_ref.dtype)


def _conformer(rest, col_u1, dw_w, dw_b, ln_g, ln_b, seq, rows):
    T = rest.shape[0]
    ktaps, C = dw_w.shape
    halo = 2 * SUBLANES
    rows = min(rows, seq)
    assert (ktaps - 1) // 2 <= halo and seq % rows == 0 and rows % halo == 0
    hb = rows // halo
    n_halo_blocks = T // halo
    main = lambda col: pl.BlockSpec((rows, C), lambda i: (i, col))
    prev = lambda col: pl.BlockSpec((halo, C), lambda i: (jnp.maximum(i * hb - 1, 0), col))
    nxt = lambda col: pl.BlockSpec((halo, C), lambda i: (jnp.minimum((i + 1) * hb, n_halo_blocks - 1), col))
    vec = pl.BlockSpec((1, C), lambda i: (0, 0))
    return pl.pallas_call(
        functools.partial(_conformer_kernel, rows=rows, halo=halo, seq=seq, ktaps=ktaps, rr=64, cc=256),
        grid=(T // rows,),
        in_specs=[main(col_u1), main(col_u1 + 1), prev(col_u1), prev(col_u1 + 1), nxt(col_u1), nxt(col_u1 + 1),
                  pl.BlockSpec((ktaps, SUBLANES, C), lambda i: (0, 0, 0)), vec, vec, vec],
        out_specs=pl.BlockSpec((rows, C), lambda i: (i, 0)),
        out_shape=jax.ShapeDtypeStruct((T, C), BF16),
        scratch_shapes=[pltpu.VMEM((SUBLANES, rows + 2 * halo, C), F32), pltpu.VMEM((rows, C), F32)],
        compiler_params=_params(1),
        name="conformer_conv",
    )(rest, rest, rest, rest, rest, rest, jnp.broadcast_to(dw_w[:, None, :], (ktaps, SUBLANES, C)),
      dw_b.reshape(1, C), ln_g.reshape(1, C), ln_b.reshape(1, C))


def _merge_kernel(ya_ref, yl_ref, yc_ref, wa_ref, wl_ref, wc_ref, g0_ref, g1_ref, g2_ref,
                  b0_ref, b1_ref, b2_ref, o_ref):
    def branch(y_ref, w_ref, g_ref, b_ref):
        return jax.nn.sigmoid(g_ref[...] + b_ref[...]) * _dot(y_ref[...], w_ref[...])

    m = branch(ya_ref, wa_ref, g0_ref, b0_ref) + branch(yl_ref, wl_ref, g1_ref, b1_ref)
    m = m + branch(yc_ref, wc_ref, g2_ref, b2_ref)
    o_ref[...] = m.astype(o_ref.dtype)


def _merge(ya, yl, yc, wa, wl, wc, rest, gate_col, gate_bias, tm, tn):
    T, D = ya.shape
    tm = min(tm, T)
    assert T % tm == 0 and D % tn == 0 and gate_col % tn == 0
    nb = D // tn
    gc = gate_col // tn
    act = pl.BlockSpec((tm, D), lambda i, j: (i, 0))
    wsp = pl.BlockSpec((D, tn), lambda i, j: (0, j))
    gate = lambda n: pl.BlockSpec((tm, tn), lambda i, j: (i, gc + n * nb + j))
    bias = lambda n: pl.BlockSpec((1, tn), lambda i, j: (0, n * nb + j))
    return pl.pallas_call(
        _merge_kernel,
        grid=(T // tm, nb),
        in_specs=[act, act, act, wsp, wsp, wsp, gate(0), gate(1), gate(2), bias(0), bias(1), bias(2)],
        out_specs=pl.BlockSpec((tm, tn), lambda i, j: (i, j)),
        out_shape=jax.ShapeDtypeStruct((T, D), BF16),
        compiler_params=_params(2),
        name="gated_merge",
    )(ya, yl, yc, wa, wl, wc, rest, rest, rest, gate_bias, gate_bias, gate_bias)


def _resmm_kernel(h_ref, w_ref, x_ref, o_ref):
    o_ref[...] = x_ref[...] + _dot(h_ref[...], w_ref[...].astype(BF16))


def _residual_matmul(h, w, layer, x, tm, tn):
    T, K = h.shape
    N = w.shape[-1]
    tm = min(tm, T)
    assert T % tm == 0 and N % tn == 0
    return pl.pallas_call(
        _resmm_kernel,
        grid=(T // tm, N // tn),
        in_specs=[
            pl.BlockSpec((tm, K), lambda i, j: (i, 0)),
            _weight_spec(w, layer, tn, lambda j: j),
            pl.BlockSpec((tm, tn), lambda i, j: (i, j)),
        ],
        out_specs=pl.BlockSpec((tm, tn), lambda i, j: (i, j)),
        out_shape=jax.ShapeDtypeStruct((T, N), F32),
        compiler_params=_params(2),
        name="residual_matmul",
    )(h, w, x)


def _xattn_kernel(x_ref, g_ref, wq_ref, kv_ref, wo_ref, o_ref, hn_ref):
    _rmsnorm_into(x_ref, g_ref, hn_ref)
    q = _dot(hn_ref[...], wq_ref[...]).astype(BF16)
    width = wq_ref.shape[1]
    hd = width // XATTN_HEADS
    scale = hd ** -0.5
    outs = []
    for h in range(XATTN_HEADS):
        cols = slice(h * hd, (h + 1) * hd)
        s = _dot_t(q[:, cols], kv_ref[:, cols]) * scale
        m = jnp.max(s, axis=-1, keepdims=True)
        p = jnp.exp(s - m)
        denom = jnp.sum(p, axis=-1, keepdims=True)
        vcols = slice(width + h * hd, width + (h + 1) * hd)
        outs.append((_dot(p.astype(BF16), kv_ref[:, vcols]) / denom).astype(BF16))
    o = jnp.concatenate(outs, axis=-1)
    o_ref[...] = x_ref[...] + _dot(o, wo_ref[...])


def _cross_attention(x, g, wq, kv, wo, seq, tm):
    T, D = x.shape
    width = wq.shape[1]
    mem_len = kv.shape[1]
    tm = min(tm, seq)
    assert seq % tm == 0
    per_seq = seq // tm
    return pl.pallas_call(
        _xattn_kernel,
        grid=(T // tm,),
        in_specs=[
            pl.BlockSpec((tm, D), lambda i: (i, 0)),
            pl.BlockSpec((1, D), lambda i: (0, 0)),
            pl.BlockSpec((D, width), lambda i: (0, 0)),
            pl.BlockSpec((None, mem_len, 2 * width), lambda i: (i // per_seq, 0, 0)),
            pl.BlockSpec((width, D), lambda i: (0, 0)),
        ],
        out_specs=pl.BlockSpec((tm, D), lambda i: (i, 0)),
        out_shape=jax.ShapeDtypeStruct((T, D), F32),
        scratch_shapes=[pltpu.VMEM((tm, D), BF16)],
        compiler_params=_params(1),
        name="memory_cross_attention",
    )(x, g.reshape(1, D), wq, kv, wo)


def _ffn_up_kernel(x_ref, g_ref, w1_ref, w3_ref, o_ref, hn_ref):
    @pl.when(pl.program_id(1) == 0)
    def _():
        _rmsnorm_into(x_ref, g_ref, hn_ref)

    hn = hn_ref[...]
    a1 = _dot(hn, w1_ref[...].astype(BF16))
    a3 = _dot(hn, w3_ref[...].astype(BF16))
    o_ref[...] = (a1 * jax.nn.sigmoid(a1) * a3).astype(o_ref.dtype)


def _ffn_up(x, g, w13, layer, tm, tn):
    T, D = x.shape
    hidden = w13.shape[-1] // 2
    tm = min(tm, T)
    assert T % tm == 0 and hidden % tn == 0
    nb = hidden // tn
    return pl.pallas_call(
        _ffn_up_kernel,
        grid=(T // tm, nb),
        in_specs=[
            pl.BlockSpec((tm, D), lambda i, j: (i, 0)),
            pl.BlockSpec((1, D), lambda i, j: (0, 0)),
            _weight_spec(w13, layer, tn, lambda j: j),
            _weight_spec(w13, layer, tn, lambda j: nb + j),
        ],
        out_specs=pl.BlockSpec((tm, tn), lambda i, j: (i, j)),
        out_shape=jax.ShapeDtypeStruct((T, hidden), BF16),
        scratch_shapes=[pltpu.VMEM((tm, D), BF16)],
        compiler_params=_params(2),
        name="ffn_up",
    )(x, g.reshape(1, D), w13, w13)


def _rmsnorm_kernel(x_ref, g_ref, o_ref):
    _rmsnorm_into(x_ref, g_ref, o_ref)


def _rmsnorm(x, g, tm):
    T, D = x.shape
    tm = min(tm, T)
    return pl.pallas_call(
        _rmsnorm_kernel,
        grid=(T // tm,),
        in_specs=[pl.BlockSpec((tm, D), lambda i: (i, 0)), pl.BlockSpec((1, D), lambda i: (0, 0))],
        out_specs=pl.BlockSpec((tm, D), lambda i: (i, 0)),
        out_shape=jax.ShapeDtypeStruct((T, D), F32),
        compiler_params=_params(1),
        name="final_rmsnorm",
    )(x, g.reshape(1, D))


def kernel(x, mem, norm_mix, w_in, gate_bias, attn_sink, lru_conv_w, lru_conv_b, lru_wr, lru_br, lru_wi, lru_bi, lru_lambda, conv_dw_w, conv_dw_b, conv_ln_g, conv_ln_b, w_proj_attn, w_proj_lru, w_proj_conv, w_out, norm_cross, norm_mem, xattn_wq, xattn_wkv, xattn_wo, norm_ffn, ffn_w13, ffn_w2, norm_final):
    B, S, D = x.shape
    M = mem.shape[1]
    T = B * S
    depth = w_in.shape[0]
    attn_w = N_HEADS * HEAD_DIM
    qkv_w = attn_w + 2 * N_KV_HEADS * HEAD_DIM
    lru_w = lru_conv_w.shape[-1]
    conv_c = conv_dw_w.shape[-1]
    rest_w = w_in.shape[2] - qkv_w
    assert lru_w == D and conv_c == D
    u_col = 2 * lru_w
    gate_col = u_col + 2 * conv_c
    slopes = jnp.exp2(-(8.0 / N_HEADS) * jnp.arange(1, N_HEADS + 1, dtype=F32))

    xf = x.reshape(T, D)
    memf = mem.reshape(B * M, D)
    bf = lambda a: a.astype(BF16)
    for l in range(depth):
        qkv = _norm_proj(xf, norm_mix[l], w_in, l, 0, qkv_w, BF16, tm=1024, tn=1024)
        rest = _norm_proj(xf, norm_mix[l], w_in, l, qkv_w, rest_w, F32, tm=1024, tn=1024)
        ya = _window_attention(qkv.reshape(B, S, qkv_w), slopes, attn_sink[l], B, S, tq=512)
        yl = _rglru(rest.reshape(B, S, rest_w), lru_conv_w[l], lru_conv_b[l], bf(lru_wr[l]), lru_br[l],
                    bf(lru_wi[l]), lru_bi[l], lru_lambda[l], B, S, rc=128)
        yc = _conformer(rest, u_col // conv_c, conv_dw_w[l], conv_dw_b[l], conv_ln_g[l], conv_ln_b[l], S, rows=256)
        merged = _merge(ya.reshape(T, attn_w), yl.reshape(T, lru_w), yc, bf(w_proj_attn[l]), bf(w_proj_lru[l]),
                        bf(w_proj_conv[l]), rest, gate_col, gate_bias[l].reshape(1, N_BRANCH * D), tm=512, tn=512)
        xf = _residual_matmul(merged, bf(w_out[l]), None, xf, tm=1024, tn=512)
        kv = _norm_proj(memf, norm_mem[l], xattn_wkv, l, 0, xattn_wkv.shape[2], BF16, tm=1024, tn=512)
        xf = _cross_attention(xf, norm_cross[l], bf(xattn_wq[l]), kv.reshape(B, M, -1), bf(xattn_wo[l]), S, tm=512)
        hidden = _ffn_up(xf, norm_ffn[l], ffn_w13, l, tm=1024, tn=512)
        xf = _residual_matmul(hidden, ffn_w2, l, xf, tm=1024, tn=256)
    return _rmsnorm(xf, norm_final, tm=512).reshape(B, S, D)
```

```python
import functools

import jax
import jax.numpy as jnp
from jax import lax
from jax.experimental import pallas as pl
from jax.experimental.pallas import tpu as pltpu

N_HEADS = 16
N_KV_HEADS = 4
HEAD_DIM = 128
WINDOW = 128
LRU_BLOCK_W = 128
LRU_CONV = 4
LRU_C = 8.0
XATTN_HEADS = 4
N_BRANCH = 3
EPS = 1e-6
NEG_INF = -1e30

SUBLANES = 8
LRU_SEG = 4
VMEM_LIMIT_BYTES = 56 * 1024 * 1024

BF16 = jnp.bfloat16
F32 = jnp.float32


def _params(n_grid_axes):
    return pltpu.CompilerParams(
        dimension_semantics=("arbitrary",) * n_grid_axes,
        vmem_limit_bytes=VMEM_LIMIT_BYTES,
    )


def _dot(a, b):
    return jnp.dot(a, b, preferred_element_type=F32)


def _dot_t(a, b):
    return lax.dot_general(a, b, (((1,), (1,)), ((), ())), preferred_element_type=F32)


def _rmsnorm_rows(x, g):
    y = x * lax.rsqrt(jnp.mean(x * x, axis=-1, keepdims=True) + EPS)
    return y * g


def _rmsnorm_into(x_ref, g_ref, out_ref, chunk=256):
    rows = x_ref.shape[0]
    chunk = min(chunk, rows)
    for r0 in range(0, rows, chunk):
        out_ref[r0:r0 + chunk, :] = _rmsnorm_rows(x_ref[r0:r0 + chunk, :], g_ref[...]).astype(out_ref.dtype)


def _proj_kernel(x_ref, g_ref, w_ref, o_ref, hn_ref):
    @pl.when(pl.program_id(1) == 0)
    def _():
        _rmsnorm_into(x_ref, g_ref, hn_ref)

    o_ref[...] = _dot(hn_ref[...], w_ref[...].astype(BF16)).astype(o_ref.dtype)


def _weight_spec(w, layer, tn, col_block_of):
    if w.ndim == 2:
        return pl.BlockSpec((w.shape[0], tn), lambda i, j: (0, col_block_of(j)))
    return pl.BlockSpec((None, w.shape[1], tn), lambda i, j: (layer, 0, col_block_of(j)))


def _norm_proj(x, g, w, layer, col_off, n_cols, out_dtype, tm, tn):
    T, K = x.shape
    tm = min(tm, T)
    assert T % tm == 0 and n_cols % tn == 0 and col_off % tn == 0
    off = col_off // tn
    return pl.pallas_call(
        _proj_kernel,
        grid=(T // tm, n_cols // tn),
        in_specs=[
            pl.BlockSpec((tm, K), lambda i, j: (i, 0)),
            pl.BlockSpec((1, K), lambda i, j: (0, 0)),
            _weight_spec(w, layer, tn, lambda j: j + off),
        ],
        out_specs=pl.BlockSpec((tm, tn), lambda i, j: (i, j)),
        out_shape=jax.ShapeDtypeStruct((T, n_cols), out_dtype),
        scratch_shapes=[pltpu.VMEM((tm, K), BF16)],
        compiler_params=_params(2),
        name="norm_proj",
    )(x, g.reshape(1, K), w)


def _attn_kernel(slopes_ref, sink_ref, q_ref, k_ref, v_ref, o_ref, *, tq, seq):
    kh = pl.program_id(1)
    win = 3 * WINDOW
    scale = HEAD_DIM ** -0.5
    grp = N_HEADS // N_KV_HEADS
    rel = (lax.broadcasted_iota(jnp.int32, (WINDOW, win), 0)
           - lax.broadcasted_iota(jnp.int32, (WINDOW, win), 1))
    for j in range(tq // WINDOW):
        rows = slice(j * WINDOW, (j + 1) * WINDOW)
        qs = pl.program_id(2) * tq + j * WINDOW
        ws = pl.multiple_of(jnp.clip(qs - WINDOW, 0, seq - win), WINDOW)
        kb = k_ref[pl.ds(ws, win), :]
        vb = v_ref[pl.ds(ws, win), :]
        dist = jnp.abs(rel + (qs - ws))
        valid = dist <= WINDOW
        distf = dist.astype(F32)
        q_stack = jnp.concatenate(
            [q_ref[rows, g * HEAD_DIM:(g + 1) * HEAD_DIM] for g in range(grp)], axis=0)
        s_all = _dot_t(q_stack, kb) * scale
        probs, inv_den = [], []
        for g in range(grp):
            h = kh * grp + g
            s = s_all[g * WINDOW:(g + 1) * WINDOW]
            s = jnp.where(valid, s - slopes_ref[h] * distf, NEG_INF)
            sink = sink_ref[h]
            m = jnp.maximum(jnp.max(s, axis=-1, keepdims=True), sink)
            p = jnp.exp(s - m)
            inv_den.append(1.0 / (jnp.sum(p, axis=-1, keepdims=True) + jnp.exp(sink - m)))
            probs.append(p.astype(BF16))
        o_all = _dot(jnp.concatenate(probs, axis=0), vb)
        for g in range(grp):
            o = o_all[g * WINDOW:(g + 1) * WINDOW] * inv_den[g]
            o_ref[rows, g * HEAD_DIM:(g + 1) * HEAD_DIM] = o.astype(o_ref.dtype)


def _window_attention(qkv, slopes, sink, batch, seq, tq):
    tq = min(tq, seq)
    assert seq % tq == 0 and tq % WINDOW == 0 and seq >= 3 * WINDOW
    grp_w = (N_HEADS // N_KV_HEADS) * HEAD_DIM
    k_blk = N_HEADS
    v_blk = N_HEADS + N_KV_HEADS
    smem = pl.BlockSpec(memory_space=pltpu.SMEM)
    return pl.pallas_call(
        functools.partial(_attn_kernel, tq=tq, seq=seq),
        grid=(batch, N_KV_HEADS, seq // tq),
        in_specs=[
            smem,
            smem,
            pl.BlockSpec((None, tq, grp_w), lambda b, h, i: (b, i, h)),
            pl.BlockSpec((None, seq, HEAD_DIM), lambda b, h, i: (b, 0, k_blk + h)),
            pl.BlockSpec((None, seq, HEAD_DIM), lambda b, h, i: (b, 0, v_blk + h)),
        ],
        out_specs=pl.BlockSpec((None, tq, grp_w), lambda b, h, i: (b, i, h)),
        out_shape=jax.ShapeDtypeStruct((batch, seq, N_HEADS * HEAD_DIM), BF16),
        compiler_params=_params(3),
        name="window_attention",
    )(slopes, sink, qkv, qkv, qkv)


def _softplus(x):
    return jnp.maximum(x, 0.0) + jnp.log1p(jnp.exp(-jnp.abs(x)))


def _lru_kernel(xl_ref, gl_ref, cw_ref, cb_ref, wr_ref, br_ref, wi_ref, bi_ref, lam_ref, o_ref,
                xp_ref, y_ref, *, seq, rc):
    pad = SUBLANES
    cb = LRU_BLOCK_W
    group = SUBLANES * LRU_SEG
    n_groups = rc // group
    n_chunks = seq // rc
    xp_ref[0:pad, :] = jnp.zeros((pad, cb), F32)
    xp_ref[pad + seq:pad + seq + pad, :] = jnp.zeros((pad, cb), F32)
    xp_ref[pad:pad + seq, :] = xl_ref[...]
    row = lax.broadcasted_iota(jnp.int32, (SUBLANES, cb), 0)

    def bcast(v):
        return jnp.broadcast_to(v, (SUBLANES, cb))

    def seg_tile(ref, start):
        return ref[pl.ds(start, SUBLANES, stride=LRU_SEG), :]

    def gates(d, t0):
        w = [bcast(cw_ref[d, k:k + 1, :]) for k in range(LRU_CONV)]
        bias = bcast(cb_ref[d:d + 1, :])
        lo = -(LRU_CONV - 1) if d == 0 else 0
        xcs = []
        for g in range(n_groups):
            base = t0 + g * group + pad
            x = {o: seg_tile(xp_ref, base + o) for o in range(lo, lo + LRU_SEG + LRU_CONV - 1)}
            for j in range(LRU_SEG):
                acc = x[lo + j] * w[0]
                for k in range(1, LRU_CONV):
                    acc = acc + x[lo + j + k] * w[k]
                xcs.append(acc + bias)
        xc = jnp.concatenate(xcs, axis=0)
        xcb = xc.astype(BF16)
        r = jax.nn.sigmoid(_dot(xcb, wr_ref[d, 0]) + br_ref[d:d + 1, :])
        ig = jax.nn.sigmoid(_dot(xcb, wi_ref[d, 0]) + bi_ref[d:d + 1, :])
        log_a = r * (-LRU_C * _softplus(-lam_ref[d:d + 1, :]))
        a = jnp.exp(log_a)
        gain = jnp.sqrt(-jnp.tanh(log_a) * (a * a + 1.0))
        return a, gain * (ig * xc)

    def tile_scan(a, b, reverse):
        for k in (1, 2, 4):
            if reverse:
                keep = row < SUBLANES - k
                sh = SUBLANES - k
            else:
                keep = row >= k
                sh = k
            a_sh = jnp.where(keep, pltpu.roll(a, sh, 0), 1.0)
            b_sh = jnp.where(keep, pltpu.roll(b, sh, 0), 0.0)
            b = a * b_sh + b
            a = a * a_sh
        return a, b

    def group_scan(a, b, g, carry, reverse):
        tile = lambda v, j: v[(g * LRU_SEG + j) * SUBLANES:(g * LRU_SEG + j + 1) * SUBLANES]
        order = list(reversed(range(LRU_SEG))) if reverse else list(range(LRU_SEG))
        hloc, prod = {}, {}
        prev = None
        for j in order:
            aj, bj = tile(a, j), tile(b, j)
            if prev is None:
                hloc[j], prod[j] = bj, aj
            else:
                hloc[j], prod[j] = aj * hloc[prev] + bj, aj * prod[prev]
            prev = j
        a_seg, b_seg = tile_scan(prod[prev], hloc[prev], reverse)
        end = a_seg * carry + b_seg
        if reverse:
            cin = jnp.where(row == SUBLANES - 1, carry, pltpu.roll(end, SUBLANES - 1, 0))
            carry = bcast(end[0:1, :])
        else:
            cin = jnp.where(row == 0, carry, pltpu.roll(end, 1, 0))
            carry = bcast(end[SUBLANES - 1:SUBLANES, :])
        return [hloc[j] + prod[j] * cin for j in range(LRU_SEG)], carry

    def fwd_chunk(c, carry):
        t0 = c * rc
        a, b = gates(0, t0)
        for g in range(n_groups):
            hs, carry = group_scan(a, b, g, carry, False)
            for j in range(LRU_SEG):
                y_ref[pl.ds(t0 + g * group + j, SUBLANES, stride=LRU_SEG), :] = hs[j]
        return carry

    unroll = min(8, n_chunks)
    assert n_chunks % unroll == 0
    lax.fori_loop(0, n_chunks, fwd_chunk, jnp.zeros((SUBLANES, cb), F32), unroll=unroll)

    def bwd_chunk(c, carry):
        t0 = (n_chunks - 1 - c) * rc
        a, b = gates(1, t0)
        for g in reversed(range(n_groups)):
            hs, carry = group_scan(a, b, g, carry, True)
            for j in range(LRU_SEG):
                start = t0 + g * group + j
                y = (seg_tile(y_ref, start) + hs[j]) * jax.nn.gelu(seg_tile(gl_ref, start))
                y_ref[pl.ds(start, SUBLANES, stride=LRU_SEG), :] = y
        return carry

    lax.fori_loop(0, n_chunks, bwd_chunk, jnp.zeros((SUBLANES, cb), F32), unroll=unroll)
    oc = min(256, seq)
    for r0 in range(0, seq, oc):
        o_ref[r0:r0 + oc, :] = y_ref[r0:r0 + oc, :].astype(o_ref.dtype)


def _rglru(rest, conv_w, conv_b, wr, br, wi, bi, lam, batch, seq, rc):
    C = conv_w.shape[-1]
    cb = LRU_BLOCK_W
    rc = min(rc, seq)
    assert C % cb == 0 and seq % rc == 0 and rc % (SUBLANES * LRU_SEG) == 0
    ncb = C // cb
    nblk = 1
    vec = pl.BlockSpec((2, cb), lambda b, c: (0, c))
    wspec = pl.BlockSpec((2, nblk, LRU_BLOCK_W, LRU_BLOCK_W), lambda b, c: (0, c, 0, 0))
    return pl.pallas_call(
        functools.partial(_lru_kernel, seq=seq, rc=rc),
        grid=(batch, ncb),
        in_specs=[
            pl.BlockSpec((None, seq, cb), lambda b, c: (b, 0, c)),
            pl.BlockSpec((None, seq, cb), lambda b, c: (b, 0, ncb + c)),
            pl.BlockSpec((2, LRU_CONV, cb), lambda b, c: (0, 0, c)),
            vec, wspec, vec, wspec, vec, vec,
        ],
        out_specs=pl.BlockSpec((None, seq, cb), lambda b, c: (b, 0, c)),
        out_shape=jax.ShapeDtypeStruct((batch, seq, C), BF16),
        scratch_shapes=[pltpu.VMEM((seq + 2 * SUBLANES, cb), F32), pltpu.VMEM((seq, cb), F32)],
        compiler_params=_params(2),
        name="rglru",
    )(rest, rest, conv_w, conv_b, wr, br, wi, bi, lam)


def _conv_quarter(u1_ref, u2_ref, u1p_ref, u2p_ref, u1n_ref, u2n_ref, w_ref, b_ref, glu_ref, sh_ref, out_ref,
                  *, rows, halo, first, last, ktaps, rr, cc, hrows):
    width = out_ref.shape[-1]
    glu_ref[halo:halo + rows, :] = u1_ref[...] * jax.nn.sigmoid(u2_ref[...])
    prev = u1p_ref[...] * jax.nn.sigmoid(u2p_ref[...])
    glu_ref[0:halo, :] = jnp.where(first, 0.0, prev)
    nxt = u1n_ref[...] * jax.nn.sigmoid(u2n_ref[...])
    glu_ref[halo + rows:halo + rows + halo, :] = jnp.where(last, 0.0, nxt)
    half = (ktaps - 1) // 2
    base = halo - half
    n_shift_rows = hrows + 2 * halo - SUBLANES
    sr = 56
    assert n_shift_rows % sr == 0 and rows % hrows == 0 and hrows % rr == 0
    for h0 in range(0, rows, hrows):
        for s in range(1, SUBLANES):
            for r0 in range(0, n_shift_rows, sr):
                for c0 in range(0, width, cc):
                    cs = slice(c0, c0 + cc)
                    blk = glu_ref[h0 + r0:h0 + r0 + sr + SUBLANES, cs]
                    sh_ref[s - 1, r0:r0 + sr, cs] = pltpu.roll(blk, sr + SUBLANES - s, 0)[0:sr]
        for r0 in range(0, hrows, rr):
            for c0 in range(0, width, cc):
                cs = slice(c0, c0 + cc)
                n_out = rr // SUBLANES
                accs = [None] * n_out
                for s in range(SUBLANES):
                    taps = [k for k in range(ktaps) if (base + k) % SUBLANES == s]
                    if not taps:
                        continue
                    t_lo = min((base + k) // SUBLANES for k in taps)
                    t_hi = max((base + k) // SUBLANES for k in taps) + n_out
                    if s == 0:
                        tiles = {i: glu_ref[h0 + r0 + i * SUBLANES:h0 + r0 + (i + 1) * SUBLANES, cs]
                                 for i in range(t_lo, t_hi)}
                    else:
                        tiles = {i: sh_ref[s - 1, r0 + i * SUBLANES:r0 + (i + 1) * SUBLANES, cs]
                                 for i in range(t_lo, t_hi)}
                    for k in taps:
                        wk = w_ref[k, :, cs]
                        a = (base + k) // SUBLANES
                        for t in range(n_out):
                            term = tiles[a + t] * wk
                            accs[t] = term if accs[t] is None else accs[t] + term
                for t in range(n_out):
                    lo = h0 + r0 + t * SUBLANES
                    out_ref[lo:lo + SUBLANES, cs] = accs[t] + b_ref[:, cs]


def _conf_merge_kernel(ya_ref, yl_ref, wa_ref, wl_ref, wc_ref, g0_ref, g1_ref, g2_ref, b0_ref, b1_ref, b2_ref,
                       u1_ref, u2_ref, u1p_ref, u2p_ref, u1n_ref, u2n_ref, dw_ref, db_ref, lg_ref, lb_ref,
                       o_ref, glu_ref, sh_ref, conv_ref, yc_ref, *, rows, halo, seq, n_tiles, ktaps, nq):
    i = pl.program_id(0)
    j = pl.program_id(1)

    @pl.when(jnp.logical_and(i == 0, j == 0))
    def _():
        yc_ref[...] = jnp.zeros(yc_ref.shape, yc_ref.dtype)

    @pl.when(jnp.logical_and(i > 0, j == 0))
    def _():
        lr = min(64, rows)
        for r0 in range(0, rows, lr):
            c = jnp.concatenate([conv_ref[q, r0:r0 + lr, :] for q in range(nq)], axis=-1)
            mu = jnp.mean(c, axis=-1, keepdims=True)
            d = c - mu
            var = jnp.mean(d * d, axis=-1, keepdims=True)
            y = d * lax.rsqrt(var + EPS) * lg_ref[...] + lb_ref[...]
            yc_ref[r0:r0 + lr, :] = (y * jax.nn.sigmoid(y)).astype(yc_ref.dtype)

    t0 = jnp.minimum(i, n_tiles - 1) * rows
    _conv_quarter(u1_ref, u2_ref, u1p_ref, u2p_ref, u1n_ref, u2n_ref, dw_ref, db_ref, glu_ref, sh_ref,
                  conv_ref.at[j], rows=rows, halo=halo, first=(t0 % seq) == 0, last=((t0 + rows) % seq) == 0,
                  ktaps=ktaps, rr=64, cc=256, hrows=min(256, rows))

    def branch(y_ref, w_ref, g_ref, b_ref):
        return jax.nn.sigmoid(g_ref[...] + b_ref[...]) * _dot(y_ref[...], w_ref[...])

    m = branch(ya_ref, wa_ref, g0_ref, b0_ref) + branch(yl_ref, wl_ref, g1_ref, b1_ref)
    m = m + branch(yc_ref, wc_ref, g2_ref, b2_ref)
    o_ref[...] = m.astype(o_ref.dtype)


def _conformer_merge(ya, yl, wa, wl, wc, rest, u_col, gate_col, gate_bias, dw_w, dw_b, ln_g, ln_b, seq, tm, tn,
                     out_lead_rows):
    T, D = ya.shape
    ktaps, C = dw_w.shape
    halo = 2 * SUBLANES
    tm = min(tm, seq)
    assert C == D and T % tm == 0 and seq % tm == 0 and D % tn == 0 and tm % halo == 0
    assert gate_col % tn == 0 and u_col % tn == 0 and (ktaps - 1) // 2 <= halo
    nq = D // tn
    n_tiles = T // tm
    gc, uc = gate_col // tn, u_col // tn
    hb = tm // halo
    n_halo_blocks = T // halo
    mt = lambda i: jnp.maximum(i - 1, 0)
    ct = lambda i: jnp.minimum(i, n_tiles - 1)
    act = pl.BlockSpec((tm, D), lambda i, j: (mt(i), 0))
    wsp = pl.BlockSpec((D, tn), lambda i, j: (0, j))
    gate = lambda n: pl.BlockSpec((tm, tn), lambda i, j: (mt(i), gc + n * nq + j))
    bias = lambda n: pl.BlockSpec((1, tn), lambda i, j: (0, n * nq + j))
    main = lambda n: pl.BlockSpec((tm, tn), lambda i, j: (ct(i), uc + n * nq + j))
    prev = lambda n: pl.BlockSpec((halo, tn), lambda i, j: (jnp.maximum(ct(i) * hb - 1, 0), uc + n * nq + j))
    nxt = lambda n: pl.BlockSpec(
        (halo, tn), lambda i, j: (jnp.minimum((ct(i) + 1) * hb, n_halo_blocks - 1), uc + n * nq + j))
    vec = pl.BlockSpec((1, C), lambda i, j: (0, 0))
    hrows = min(256, tm)
    lead = out_lead_rows // tm
    assert out_lead_rows % tm == 0 and lead >= 1
    return pl.pallas_call(
        functools.partial(_conf_merge_kernel, rows=tm, halo=halo, seq=seq, n_tiles=n_tiles, ktaps=ktaps, nq=nq),
        grid=(n_tiles + 1, nq),
        in_specs=[act, act, wsp, wsp, wsp, gate(0), gate(1), gate(2), bias(0), bias(1), bias(2),
                  main(0), main(1), prev(0), prev(1), nxt(0), nxt(1),
                  pl.BlockSpec((ktaps, SUBLANES, tn), lambda i, j: (0, 0, j)),
                  pl.BlockSpec((1, tn), lambda i, j: (0, j)), vec, vec],
        out_specs=pl.BlockSpec((tm, tn), lambda i, j: (i + lead - 1, j)),
        out_shape=jax.ShapeDtypeStruct((T + lead * tm, D), BF16),
        scratch_shapes=[pltpu.VMEM((tm + 2 * halo, tn), F32),
                        pltpu.VMEM((SUBLANES - 1, hrows + 2 * halo - SUBLANES, tn), F32),
                        pltpu.VMEM((nq, tm, tn), F32),
                        pltpu.VMEM((tm, D), BF16)],
        compiler_params=_params(2),
        name="conformer_merge",
    )(ya, yl, wa, wl, wc, rest, rest, rest, gate_bias, gate_bias, gate_bias,
      rest, rest, rest, rest, rest, rest, jnp.broadcast_to(dw_w[:, None, :], (ktaps, SUBLANES, C)),
      dw_b.reshape(1, C), ln_g.reshape(1, C), ln_b.reshape(1, C))


def _resmm_kernel(h_ref, w_ref, x_ref, o_ref):
    o_ref[...] = x_ref[...] + _dot(h_ref[...], w_ref[...].astype(BF16))


def _residual_matmul(h, w, layer, x, tm, tn, h_lead_rows=0):
    T, K = x.shape[0], h.shape[1]
    N = w.shape[-1]
    tm = min(tm, T)
    assert T % tm == 0 and N % tn == 0 and h_lead_rows % tm == 0
    lead = h_lead_rows // tm
    return pl.pallas_call(
        _resmm_kernel,
        grid=(T // tm, N // tn),
        in_specs=[
            pl.BlockSpec((tm, K), lambda i, j: (i + lead, 0)),
            _weight_spec(w, layer, tn, lambda j: j),
            pl.BlockSpec((tm, tn), lambda i, j: (i, j)),
        ],
        out_specs=pl.BlockSpec((tm, tn), lambda i, j: (i, j)),
        out_shape=jax.ShapeDtypeStruct((T, N), F32),
        compiler_params=_params(2),
        name="residual_matmul",
    )(h, w, x)


def _xattn_kernel(x_ref, g_ref, wq_ref, kv_ref, wo_ref, o_ref, hn_ref):
    _rmsnorm_into(x_ref, g_ref, hn_ref)
    q = _dot(hn_ref[...], wq_ref[...]).astype(BF16)
    width = wq_ref.shape[1]
    hd = width // XATTN_HEADS
    scale = hd ** -0.5
    outs = []
    for h in range(XATTN_HEADS):
        cols = slice(h * hd, (h + 1) * hd)
        s = _dot_t(q[:, cols], kv_ref[:, cols]) * scale
        m = jnp.max(s, axis=-1, keepdims=True)
        p = jnp.exp(s - m)
        denom = jnp.sum(p, axis=-1, keepdims=True)
        vcols = slice(width + h * hd, width + (h + 1) * hd)
        outs.append((_dot(p.astype(BF16), kv_ref[:, vcols]) / denom).astype(BF16))
    o = jnp.concatenate(outs, axis=-1)
    o_ref[...] = x_ref[...] + _dot(o, wo_ref[...])


def _cross_attention(x, g, wq, kv, wo, seq, tm):
    T, D = x.shape
    width = wq.shape[1]
    mem_len = kv.shape[1]
    tm = min(tm, seq)
    assert seq % tm == 0
    per_seq = seq // tm
    return pl.pallas_call(
        _xattn_kernel,
        grid=(T // tm,),
        in_specs=[
            pl.BlockSpec((tm, D), lambda i: (i, 0)),
            pl.BlockSpec((1, D), lambda i: (0, 0)),
            pl.BlockSpec((D, width), lambda i: (0, 0)),
            pl.BlockSpec((None, mem_len, 2 * width), lambda i: (i // per_seq, 0, 0)),
            pl.BlockSpec((width, D), lambda i: (0, 0)),
        ],
        out_specs=pl.BlockSpec((tm, D), lambda i: (i, 0)),
        out_shape=jax.ShapeDtypeStruct((T, D), F32),
        scratch_shapes=[pltpu.VMEM((tm, D), BF16)],
        compiler_params=_params(1),
        name="memory_cross_attention",
    )(x, g.reshape(1, D), wq, kv, wo)


def _ffn_up_kernel(x_ref, g_ref, w1_ref, w3_ref, o_ref, hn_ref):
    @pl.when(pl.program_id(1) == 0)
    def _():
        _rmsnorm_into(x_ref, g_ref, hn_ref)

    hn = hn_ref[...]
    a1 = _dot(hn, w1_ref[...].astype(BF16))
    a3 = _dot(hn, w3_ref[...].astype(BF16))
    o_ref[...] = (a1 * jax.nn.sigmoid(a1) * a3).astype(o_ref.dtype)


def _ffn_up(x, g, w13, layer, tm, tn):
    T, D = x.shape
    hidden = w13.shape[-1] // 2
    tm = min(tm, T)
    assert T % tm == 0 and hidden % tn == 0
    nb = hidden // tn
    return pl.pallas_call(
        _ffn_up_kernel,
        grid=(T // tm, nb),
        in_specs=[
            pl.BlockSpec((tm, D), lambda i, j: (i, 0)),
            pl.BlockSpec((1, D), lambda i, j: (0, 0)),
            _weight_spec(w13, layer, tn, lambda j: j),
            _weight_spec(w13, layer, tn, lambda j: nb + j),
        ],
        out_specs=pl.BlockSpec((tm, tn), lambda i, j: (i, j)),
        out_shape=jax.ShapeDtypeStruct((T, hidden), BF16),
        scratch_shapes=[pltpu.VMEM((tm, D), BF16)],
        compiler_params=_params(2),
        name="ffn_up",
    )(x, g.reshape(1, D), w13, w13)


def _rmsnorm_kernel(x_ref, g_ref, o_ref):
    _rmsnorm_into(x_ref, g_ref, o_ref)


def _rmsnorm(x, g, tm):
    T, D = x.shape
    tm = min(tm, T)
    return pl.pallas_call(
        _rmsnorm_kernel,
        grid=(T // tm,),
        in_specs=[pl.BlockSpec((tm, D), lambda i: (i, 0)), pl.BlockSpec((1, D), lambda i: (0, 0))],
        out_specs=pl.BlockSpec((tm, D), lambda i: (i, 0)),
        out_shape=jax.ShapeDtypeStruct((T, D), F32),
        compiler_params=_params(1),
        name="final_rmsnorm",
    )(x, g.reshape(1, D))


def kernel(x, mem, norm_mix, w_in, gate_bias, attn_sink, lru_conv_w, lru_conv_b, lru_wr, lru_br, lru_wi, lru_bi, lru_lambda, conv_dw_w, conv_dw_b, conv_ln_g, conv_ln_b, w_proj_attn, w_proj_lru, w_proj_conv, w_out, norm_cross, norm_mem, xattn_wq, xattn_wkv, xattn_wo, norm_ffn, ffn_w13, ffn_w2, norm_final):
    B, S, D = x.shape
    M = mem.shape[1]
    T = B * S
    depth = w_in.shape[0]
    attn_w = N_HEADS * HEAD_DIM
    qkv_w = attn_w + 2 * N_KV_HEADS * HEAD_DIM
    lru_w = lru_conv_w.shape[-1]
    conv_c = conv_dw_w.shape[-1]
    rest_w = w_in.shape[2] - qkv_w
    assert lru_w == D and conv_c == D
    u_col = 2 * lru_w
    gate_col = u_col + 2 * conv_c
    slopes = jnp.exp2(-(8.0 / N_HEADS) * jnp.arange(1, N_HEADS + 1, dtype=F32))

    xf = x.reshape(T, D)
    memf = mem.reshape(B * M, D)
    bf = lambda a: a.astype(BF16)
    for l in range(depth):
        qkv = _norm_proj(xf, norm_mix[l], w_in, l, 0, qkv_w, BF16, tm=1024, tn=1024)
        rest = _norm_proj(xf, norm_mix[l], w_in, l, qkv_w, rest_w, F32, tm=1024, tn=1024)
        ya = _window_attention(qkv.reshape(B, S, qkv_w), slopes, attn_sink[l], B, S, tq=512)
        yl = _rglru(rest.reshape(B, S, rest_w), lru_conv_w[l], lru_conv_b[l], bf(lru_wr[l]), lru_br[l],
                    bf(lru_wi[l]), lru_bi[l], lru_lambda[l], B, S, rc=128)
        merged = _conformer_merge(ya.reshape(T, attn_w), yl.reshape(T, lru_w), bf(w_proj_attn[l]), bf(w_proj_lru[l]),
                                  bf(w_proj_conv[l]), rest, u_col, gate_col, gate_bias[l].reshape(1, N_BRANCH * D),
                                  conv_dw_w[l], conv_dw_b[l], conv_ln_g[l], conv_ln_b[l], S, tm=512, tn=512,
                                  out_lead_rows=1024)
        xf = _residual_matmul(merged, bf(w_out[l]), None, xf, tm=1024, tn=1024, h_lead_rows=1024)
        kv = _norm_proj(memf, norm_mem[l], xattn_wkv, l, 0, xattn_wkv.shape[2], BF16, tm=1024, tn=512)
        xf = _cross_attention(xf, norm_cross[l], bf(xattn_wq[l]), kv.reshape(B, M, -1), bf(xattn_wo[l]), S, tm=512)
        hidden = _ffn_up(xf, norm_ffn[l], ffn_w13, l, tm=1024, tn=512)
        xf = _residual_matmul(hidden, ffn_w2, l, xf, tm=1024, tn=256)
    return _rmsnorm(xf, norm_final, tm=512).reshape(B, S, D)
```

```python
import functools

import jax
import jax.numpy as jnp
from jax import lax
from jax.experimental import pallas as pl
from jax.experimental.pallas import tpu as pltpu

N_HEADS = 16
N_KV_HEADS = 4
HEAD_DIM = 128
WINDOW = 128
LRU_BLOCK_W = 128
LRU_CONV = 4
LRU_C = 8.0
XATTN_HEADS = 4
N_BRANCH = 3
EPS = 1e-6
NEG_INF = -1e30

SUBLANES = 8
LRU_SEG = 4
VMEM_LIMIT_BYTES = 56 * 1024 * 1024

BF16 = jnp.bfloat16
F32 = jnp.float32


def _params(n_grid_axes):
    return pltpu.CompilerParams(
        dimension_semantics=("arbitrary",) * n_grid_axes,
        vmem_limit_bytes=VMEM_LIMIT_BYTES,
    )


def _dot(a, b):
    return jnp.dot(a, b, preferred_element_type=F32)


def _dot_t(a, b):
    return lax.dot_general(a, b, (((1,), (1,)), ((), ())), preferred_element_type=F32)


def _rmsnorm_rows(x, g):
    y = x * lax.rsqrt(jnp.mean(x * x, axis=-1, keepdims=True) + EPS)
    return y * g


def _rmsnorm_into(x_ref, g_ref, out_ref, chunk=256):
    rows = x_ref.shape[0]
    chunk = min(chunk, rows)
    for r0 in range(0, rows, chunk):
        out_ref[r0:r0 + chunk, :] = _rmsnorm_rows(x_ref[r0:r0 + chunk, :], g_ref[...]).astype(out_ref.dtype)


def _proj_kernel(x_ref, g_ref, w_ref, o_ref, hn_ref):
    @pl.when(pl.program_id(1) == 0)
    def _():
        _rmsnorm_into(x_ref, g_ref, hn_ref)

    o_ref[...] = _dot(hn_ref[...], w_ref[...].astype(BF16)).astype(o_ref.dtype)


def _weight_spec(w, layer, tn, col_block_of):
    if w.ndim == 2:
        return pl.BlockSpec((w.shape[0], tn), lambda i, j: (0, col_block_of(j)))
    return pl.BlockSpec((None, w.shape[1], tn), lambda i, j: (layer, 0, col_block_of(j)))


def _norm_proj(x, g, w, layer, col_off, n_cols, out_dtype, tm, tn):
    T, K = x.shape
    tm = min(tm, T)
    assert T % tm == 0 and n_cols % tn == 0 and col_off % tn == 0
    off = col_off // tn
    return pl.pallas_call(
        _proj_kernel,
        grid=(T // tm, n_cols // tn),
        in_specs=[
            pl.BlockSpec((tm, K), lambda i, j: (i, 0)),
            pl.BlockSpec((1, K), lambda i, j: (0, 0)),
            _weight_spec(w, layer, tn, lambda j: j + off),
        ],
        out_specs=pl.BlockSpec((tm, tn), lambda i, j: (i, j)),
        out_shape=jax.ShapeDtypeStruct((T, n_cols), out_dtype),
        scratch_shapes=[pltpu.VMEM((tm, K), BF16)],
        compiler_params=_params(2),
        name="norm_proj",
    )(x, g.reshape(1, K), w)


def _proj_wres_kernel(hn_ref, w_ref, o_ref, wb_ref):
    @pl.when(pl.program_id(1) == 0)
    def _():
        rows = w_ref.shape[0]
        chunk = min(256, rows)
        for r0 in range(0, rows, chunk):
            wb_ref[r0:r0 + chunk, :] = w_ref[r0:r0 + chunk, :].astype(wb_ref.dtype)

    o_ref[...] = _dot(hn_ref[...], wb_ref[...]).astype(o_ref.dtype)


def _proj_weight_resident(hn, w, layer, col_off, n_cols, out_dtype, tm, tn):
    T, K = hn.shape
    tm = min(tm, T)
    assert T % tm == 0 and n_cols % tn == 0 and col_off % tn == 0
    off = col_off // tn
    return pl.pallas_call(
        _proj_wres_kernel,
        grid=(n_cols // tn, T // tm),
        in_specs=[
            pl.BlockSpec((tm, K), lambda j, i: (i, 0)),
            pl.BlockSpec((None, K, tn), lambda j, i: (layer, 0, j + off)),
        ],
        out_specs=pl.BlockSpec((tm, tn), lambda j, i: (i, j)),
        out_shape=jax.ShapeDtypeStruct((T, n_cols), out_dtype),
        scratch_shapes=[pltpu.VMEM((K, tn), BF16)],
        compiler_params=_params(2),
        name="proj_weight_resident",
    )(hn, w)


def _attn_kernel(slopes_ref, sink_ref, q_ref, k_ref, v_ref, o_ref, *, tq, seq):
    kh = pl.program_id(1)
    win = 3 * WINDOW
    scale = HEAD_DIM ** -0.5
    grp = N_HEADS // N_KV_HEADS
    rel = (lax.broadcasted_iota(jnp.int32, (WINDOW, win), 0)
           - lax.broadcasted_iota(jnp.int32, (WINDOW, win), 1))
    for j in range(tq // WINDOW):
        rows = slice(j * WINDOW, (j + 1) * WINDOW)
        qs = pl.program_id(2) * tq + j * WINDOW
        ws = pl.multiple_of(jnp.clip(qs - WINDOW, 0, seq - win), WINDOW)
        kb = k_ref[pl.ds(ws, win), :]
        vb = v_ref[pl.ds(ws, win), :]
        dist = jnp.abs(rel + (qs - ws))
        valid = dist <= WINDOW
        distf = dist.astype(F32)
        q_stack = jnp.concatenate(
            [q_ref[rows, g * HEAD_DIM:(g + 1) * HEAD_DIM] for g in range(grp)], axis=0)
        s_all = _dot_t(q_stack, kb) * scale
        probs, inv_den = [], []
        for g in range(grp):
            h = kh * grp + g
            s = s_all[g * WINDOW:(g + 1) * WINDOW]
            s = jnp.where(valid, s - slopes_ref[h] * distf, NEG_INF)
            sink = sink_ref[h]
            m = jnp.maximum(jnp.max(s, axis=-1, keepdims=True), sink)
            p = jnp.exp(s - m)
            inv_den.append(1.0 / (jnp.sum(p, axis=-1, keepdims=True) + jnp.exp(sink - m)))
            probs.append(p.astype(BF16))
        o_all = _dot(jnp.concatenate(probs, axis=0), vb)
        for g in range(grp):
            o = o_all[g * WINDOW:(g + 1) * WINDOW] * inv_den[g]
            o_ref[rows, g * HEAD_DIM:(g + 1) * HEAD_DIM] = o.astype(o_ref.dtype)


def _window_attention(qkv, slopes, sink, batch, seq, tq):
    tq = min(tq, seq)
    assert seq % tq == 0 and tq % WINDOW == 0 and seq >= 3 * WINDOW
    grp_w = (N_HEADS // N_KV_HEADS) * HEAD_DIM
    k_blk = N_HEADS
    v_blk = N_HEADS + N_KV_HEADS
    smem = pl.BlockSpec(memory_space=pltpu.SMEM)
    return pl.pallas_call(
        functools.partial(_attn_kernel, tq=tq, seq=seq),
        grid=(batch, N_KV_HEADS, seq // tq),
        in_specs=[
            smem,
            smem,
            pl.BlockSpec((None, tq, grp_w), lambda b, h, i: (b, i, h)),
            pl.BlockSpec((None, seq, HEAD_DIM), lambda b, h, i: (b, 0, k_blk + h)),
            pl.BlockSpec((None, seq, HEAD_DIM), lambda b, h, i: (b, 0, v_blk + h)),
        ],
        out_specs=pl.BlockSpec((None, tq, grp_w), lambda b, h, i: (b, i, h)),
        out_shape=jax.ShapeDtypeStruct((batch, seq, N_HEADS * HEAD_DIM), BF16),
        compiler_params=_params(3),
        name="window_attention",
    )(slopes, sink, qkv, qkv, qkv)


def _softplus(x):
    return jnp.maximum(x, 0.0) + jnp.log1p(jnp.exp(-jnp.abs(x)))


def _lru_kernel(xl_ref, gl_ref, cw_ref, cb_ref, wr_ref, br_ref, wi_ref, bi_ref, lam_ref, o_ref,
                xp_ref, y_ref, *, seq, rc):
    pad = SUBLANES
    cb = LRU_BLOCK_W
    group = SUBLANES * LRU_SEG
    n_groups = rc // group
    n_chunks = seq // rc
    xp_ref[0:pad, :] = jnp.zeros((pad, cb), F32)
    xp_ref[pad + seq:pad + seq + pad, :] = jnp.zeros((pad, cb), F32)
    xp_ref[pad:pad + seq, :] = xl_ref[...]
    row = lax.broadcasted_iota(jnp.int32, (SUBLANES, cb), 0)

    def bcast(v):
        return jnp.broadcast_to(v, (SUBLANES, cb))

    def seg_tile(ref, start):
        return ref[pl.ds(start, SUBLANES, stride=LRU_SEG), :]

    def gates(d, t0):
        w = [bcast(cw_ref[d, k:k + 1, :]) for k in range(LRU_CONV)]
        bias = bcast(cb_ref[d:d + 1, :])
        lo = -(LRU_CONV - 1) if d == 0 else 0
        xcs = []
        for g in range(n_groups):
            base = t0 + g * group + pad
            x = {o: seg_tile(xp_ref, base + o) for o in range(lo, lo + LRU_SEG + LRU_CONV - 1)}
            for j in range(LRU_SEG):
                acc = x[lo + j] * w[0]
                for k in range(1, LRU_CONV):
                    acc = acc + x[lo + j + k] * w[k]
                xcs.append(acc + bias)
        xc = jnp.concatenate(xcs, axis=0)
        xcb = xc.astype(BF16)
        r = jax.nn.sigmoid(_dot(xcb, wr_ref[d, 0]) + br_ref[d:d + 1, :])
        ig = jax.nn.sigmoid(_dot(xcb, wi_ref[d, 0]) + bi_ref[d:d + 1, :])
        log_a = r * (-LRU_C * _softplus(-lam_ref[d:d + 1, :]))
        a = jnp.exp(log_a)
        gain = jnp.sqrt(-jnp.tanh(log_a) * (a * a + 1.0))
        return a, gain * (ig * xc)

    def tile_scan(a, b, reverse):
        for k in (1, 2, 4):
            if reverse:
                keep = row < SUBLANES - k
                sh = SUBLANES - k
            else:
                keep = row >= k
                sh = k
            a_sh = jnp.where(keep, pltpu.roll(a, sh, 0), 1.0)
            b_sh = jnp.where(keep, pltpu.roll(b, sh, 0), 0.0)
            b = a * b_sh + b
            a = a * a_sh
        return a, b

    def group_scan(a, b, g, carry, reverse):
        tile = lambda v, j: v[(g * LRU_SEG + j) * SUBLANES:(g * LRU_SEG + j + 1) * SUBLANES]
        order = list(reversed(range(LRU_SEG))) if reverse else list(range(LRU_SEG))
        hloc, prod = {}, {}
        prev = None
        for j in order:
            aj, bj = tile(a, j), tile(b, j)
            if prev is None:
                hloc[j], prod[j] = bj, aj
            else:
                hloc[j], prod[j] = aj * hloc[prev] + bj, aj * prod[prev]
            prev = j
        a_seg, b_seg = tile_scan(prod[prev], hloc[prev], reverse)
        end = a_seg * carry + b_seg
        if reverse:
            cin = jnp.where(row == SUBLANES - 1, carry, pltpu.roll(end, SUBLANES - 1, 0))
            carry = bcast(end[0:1, :])
        else:
            cin = jnp.where(row == 0, carry, pltpu.roll(end, 1, 0))
            carry = bcast(end[SUBLANES - 1:SUBLANES, :])
        return [hloc[j] + prod[j] * cin for j in range(LRU_SEG)], carry

    def fwd_chunk(c, carry):
        t0 = c * rc
        a, b = gates(0, t0)
        for g in range(n_groups):
            hs, carry = group_scan(a, b, g, carry, False)
            for j in range(LRU_SEG):
                y_ref[pl.ds(t0 + g * group + j, SUBLANES, stride=LRU_SEG), :] = hs[j]
        return carry

    unroll = min(8, n_chunks)
    assert n_chunks % unroll == 0
    lax.fori_loop(0, n_chunks, fwd_chunk, jnp.zeros((SUBLANES, cb), F32), unroll=unroll)

    def bwd_chunk(c, carry):
        t0 = (n_chunks - 1 - c) * rc
        a, b = gates(1, t0)
        for g in reversed(range(n_groups)):
            hs, carry = group_scan(a, b, g, carry, True)
            for j in range(LRU_SEG):
                start = t0 + g * group + j
                y = (seg_tile(y_ref, start) + hs[j]) * jax.nn.gelu(seg_tile(gl_ref, start))
                y_ref[pl.ds(start, SUBLANES, stride=LRU_SEG), :] = y
        return carry

    lax.fori_loop(0, n_chunks, bwd_chunk, jnp.zeros((SUBLANES, cb), F32), unroll=unroll)
    oc = min(256, seq)
    for r0 in range(0, seq, oc):
        o_ref[r0:r0 + oc, :] = y_ref[r0:r0 + oc, :].astype(o_ref.dtype)


def _rglru(rest, conv_w, conv_b, wr, br, wi, bi, lam, batch, seq, rc):
    C = conv_w.shape[-1]
    cb = LRU_BLOCK_W
    rc = min(rc, seq)
    assert C % cb == 0 and seq % rc == 0 and rc % (SUBLANES * LRU_SEG) == 0
    ncb = C // cb
    nblk = 1
    vec = pl.BlockSpec((2, cb), lambda b, c: (0, c))
    wspec = pl.BlockSpec((2, nblk, LRU_BLOCK_W, LRU_BLOCK_W), lambda b, c: (0, c, 0, 0))
    return pl.pallas_call(
        functools.partial(_lru_kernel, seq=seq, rc=rc),
        grid=(batch, ncb),
        in_specs=[
            pl.BlockSpec((None, seq, cb), lambda b, c: (b, 0, c)),
            pl.BlockSpec((None, seq, cb), lambda b, c: (b, 0, ncb + c)),
            pl.BlockSpec((2, LRU_CONV, cb), lambda b, c: (0, 0, c)),
            vec, wspec, vec, wspec, vec, vec,
        ],
        out_specs=pl.BlockSpec((None, seq, cb), lambda b, c: (b, 0, c)),
        out_shape=jax.ShapeDtypeStruct((batch, seq, C), BF16),
        scratch_shapes=[pltpu.VMEM((seq + 2 * SUBLANES, cb), F32), pltpu.VMEM((seq, cb), F32)],
        compiler_params=_params(2),
        name="rglru",
    )(rest, rest, conv_w, conv_b, wr, br, wi, bi, lam)


def _conv_quarter(u1_ref, u2_ref, u1p_ref, u2p_ref, u1n_ref, u2n_ref, w_ref, b_ref, glu_ref, sh_ref, out_ref,
                  *, rows, halo, first, last, ktaps, rr, cc, hrows):
    width = out_ref.shape[-1]
    glu_ref[halo:halo + rows, :] = u1_ref[...] * jax.nn.sigmoid(u2_ref[...])
    prev = u1p_ref[...] * jax.nn.sigmoid(u2p_ref[...])
    glu_ref[0:halo, :] = jnp.where(first, 0.0, prev)
    nxt = u1n_ref[...] * jax.nn.sigmoid(u2n_ref[...])
    glu_ref[halo + rows:halo + rows + halo, :] = jnp.where(last, 0.0, nxt)
    half = (ktaps - 1) // 2
    base = halo - half
    n_shift_rows = hrows + 2 * halo - SUBLANES
    sr = 56
    assert n_shift_rows % sr == 0 and rows % hrows == 0 and hrows % rr == 0
    for h0 in range(0, rows, hrows):
        for s in range(1, SUBLANES):
            for r0 in range(0, n_shift_rows, sr):
                for c0 in range(0, width, cc):
                    cs = slice(c0, c0 + cc)
                    blk = glu_ref[h0 + r0:h0 + r0 + sr + SUBLANES, cs]
                    sh_ref[s - 1, r0:r0 + sr, cs] = pltpu.roll(blk, sr + SUBLANES - s, 0)[0:sr]
        for r0 in range(0, hrows, rr):
            for c0 in range(0, width, cc):
                cs = slice(c0, c0 + cc)
                n_out = rr // SUBLANES
                accs = [None] * n_out
                for s in range(SUBLANES):
                    taps = [k for k in range(ktaps) if (base + k) % SUBLANES == s]
                    if not taps:
                        continue
                    t_lo = min((base + k) // SUBLANES for k in taps)
                    t_hi = max((base + k) // SUBLANES for k in taps) + n_out
                    if s == 0:
                        tiles = {i: glu_ref[h0 + r0 + i * SUBLANES:h0 + r0 + (i + 1) * SUBLANES, cs]
                                 for i in range(t_lo, t_hi)}
                    else:
                        tiles = {i: sh_ref[s - 1, r0 + i * SUBLANES:r0 + (i + 1) * SUBLANES, cs]
                                 for i in range(t_lo, t_hi)}
                    for k in taps:
                        wk = w_ref[k, :, cs]
                        a = (base + k) // SUBLANES
                        for t in range(n_out):
                            term = tiles[a + t] * wk
                            accs[t] = term if accs[t] is None else accs[t] + term
                for t in range(n_out):
                    lo = h0 + r0 + t * SUBLANES
                    out_ref[lo:lo + SUBLANES, cs] = accs[t] + b_ref[:, cs]


def _conf_merge_kernel(ya_ref, yl_ref, wa_ref, wl_ref, wc_ref, g0_ref, g1_ref, g2_ref, b0_ref, b1_ref, b2_ref,
                       u1_ref, u2_ref, u1p_ref, u2p_ref, u1n_ref, u2n_ref, dw_ref, db_ref, lg_ref, lb_ref,
                       o_ref, glu_ref, sh_ref, conv_ref, yc_ref, *, rows, halo, seq, n_tiles, ktaps, nq):
    i = pl.program_id(0)
    j = pl.program_id(1)

    @pl.when(jnp.logical_and(i == 0, j == 0))
    def _():
        yc_ref[...] = jnp.zeros(yc_ref.shape, yc_ref.dtype)

    @pl.when(jnp.logical_and(i > 0, j == 0))
    def _():
        lr = min(64, rows)
        for r0 in range(0, rows, lr):
            c = jnp.concatenate([conv_ref[q, r0:r0 + lr, :] for q in range(nq)], axis=-1)
            mu = jnp.mean(c, axis=-1, keepdims=True)
            d = c - mu
            var = jnp.mean(d * d, axis=-1, keepdims=True)
            y = d * lax.rsqrt(var + EPS) * lg_ref[...] + lb_ref[...]
            yc_ref[r0:r0 + lr, :] = (y * jax.nn.sigmoid(y)).astype(yc_ref.dtype)

    t0 = jnp.minimum(i, n_tiles - 1) * rows
    _conv_quarter(u1_ref, u2_ref, u1p_ref, u2p_ref, u1n_ref, u2n_ref, dw_ref, db_ref, glu_ref, sh_ref,
                  conv_ref.at[j], rows=rows, halo=halo, first=(t0 % seq) == 0, last=((t0 + rows) % seq) == 0,
                  ktaps=ktaps, rr=64, cc=256, hrows=min(256, rows))

    def branch(y_ref, w_ref, g_ref, b_ref):
        return jax.nn.sigmoid(g_ref[...] + b_ref[...]) * _dot(y_ref[...], w_ref[...])

    m = branch(ya_ref, wa_ref, g0_ref, b0_ref) + branch(yl_ref, wl_ref, g1_ref, b1_ref)
    m = m + branch(yc_ref, wc_ref, g2_ref, b2_ref)
    o_ref[...] = m.astype(o_ref.dtype)


def _conformer_merge(ya, yl, wa, wl, wc, rest, u_col, gate_col, gate_bias, dw_w, dw_b, ln_g, ln_b, seq, tm, tn,
                     out_lead_rows):
    T, D = ya.shape
    ktaps, C = dw_w.shape
    halo = 2 * SUBLANES
    tm = min(tm, seq)
    assert C == D and T % tm == 0 and seq % tm == 0 and D % tn == 0 and tm % halo == 0
    assert gate_col % tn == 0 and u_col % tn == 0 and (ktaps - 1) // 2 <= halo
    nq = D // tn
    n_tiles = T // tm
    gc, uc = gate_col // tn, u_col // tn
    hb = tm // halo
    n_halo_blocks = T // halo
    mt = lambda i: jnp.maximum(i - 1, 0)
    ct = lambda i: jnp.minimum(i, n_tiles - 1)
    act = pl.BlockSpec((tm, D), lambda i, j: (mt(i), 0))
    wsp = pl.BlockSpec((D, tn), lambda i, j: (0, j))
    gate = lambda n: pl.BlockSpec((tm, tn), lambda i, j: (mt(i), gc + n * nq + j))
    bias = lambda n: pl.BlockSpec((1, tn), lambda i, j: (0, n * nq + j))
    main = lambda n: pl.BlockSpec((tm, tn), lambda i, j: (ct(i), uc + n * nq + j))
    prev = lambda n: pl.BlockSpec((halo, tn), lambda i, j: (jnp.maximum(ct(i) * hb - 1, 0), uc + n * nq + j))
    nxt = lambda n: pl.BlockSpec(
        (halo, tn), lambda i, j: (jnp.minimum((ct(i) + 1) * hb, n_halo_blocks - 1), uc + n * nq + j))
    vec = pl.BlockSpec((1, C), lambda i, j: (0, 0))
    hrows = min(256, tm)
    lead = out_lead_rows // tm
    assert out_lead_rows % tm == 0 and lead >= 1
    return pl.pallas_call(
        functools.partial(_conf_merge_kernel, rows=tm, halo=halo, seq=seq, n_tiles=n_tiles, ktaps=ktaps, nq=nq),
        grid=(n_tiles + 1, nq),
        in_specs=[act, act, wsp, wsp, wsp, gate(0), gate(1), gate(2), bias(0), bias(1), bias(2),
                  main(0), main(1), prev(0), prev(1), nxt(0), nxt(1),
                  pl.BlockSpec((ktaps, SUBLANES, tn), lambda i, j: (0, 0, j)),
                  pl.BlockSpec((1, tn), lambda i, j: (0, j)), vec, vec],
        out_specs=pl.BlockSpec((tm, tn), lambda i, j: (i + lead - 1, j)),
        out_shape=jax.ShapeDtypeStruct((T + lead * tm, D), BF16),
        scratch_shapes=[pltpu.VMEM((tm + 2 * halo, tn), F32),
                        pltpu.VMEM((SUBLANES - 1, hrows + 2 * halo - SUBLANES, tn), F32),
                        pltpu.VMEM((nq, tm, tn), F32),
                        pltpu.VMEM((tm, D), BF16)],
        compiler_params=_params(2),
        name="conformer_merge",
    )(ya, yl, wa, wl, wc, rest, rest, rest, gate_bias, gate_bias, gate_bias,
      rest, rest, rest, rest, rest, rest, jnp.broadcast_to(dw_w[:, None, :], (ktaps, SUBLANES, C)),
      dw_b.reshape(1, C), ln_g.reshape(1, C), ln_b.reshape(1, C))


def _resmm_kernel(h_ref, w_ref, x_ref, o_ref):
    o_ref[...] = x_ref[...] + _dot(h_ref[...], w_ref[...].astype(BF16))


def _residual_matmul(h, w, layer, x, tm, tn, h_lead_rows=0):
    T, K = x.shape[0], h.shape[1]
    N = w.shape[-1]
    tm = min(tm, T)
    assert T % tm == 0 and N % tn == 0 and h_lead_rows % tm == 0
    lead = h_lead_rows // tm
    return pl.pallas_call(
        _resmm_kernel,
        grid=(T // tm, N // tn),
        in_specs=[
            pl.BlockSpec((tm, K), lambda i, j: (i + lead, 0)),
            _weight_spec(w, layer, tn, lambda j: j),
            pl.BlockSpec((tm, tn), lambda i, j: (i, j)),
        ],
        out_specs=pl.BlockSpec((tm, tn), lambda i, j: (i, j)),
        out_shape=jax.ShapeDtypeStruct((T, N), F32),
        compiler_params=_params(2),
        name="residual_matmul",
    )(h, w, x)


def _xattn_kernel(x_ref, g_ref, wq_ref, kv_ref, wo_ref, gn_ref, o_ref, hnext_ref, hn_ref):
    _rmsnorm_into(x_ref, g_ref, hn_ref)
    q = _dot(hn_ref[...], wq_ref[...]).astype(BF16)
    width = wq_ref.shape[1]
    hd = width // XATTN_HEADS
    scale = hd ** -0.5
    outs = []
    for h in range(XATTN_HEADS):
        cols = slice(h * hd, (h + 1) * hd)
        s = _dot_t(q[:, cols], kv_ref[:, cols]) * scale
        m = jnp.max(s, axis=-1, keepdims=True)
        p = jnp.exp(s - m)
        denom = jnp.sum(p, axis=-1, keepdims=True)
        vcols = slice(width + h * hd, width + (h + 1) * hd)
        outs.append((_dot(p.astype(BF16), kv_ref[:, vcols]) / denom).astype(BF16))
    o = jnp.concatenate(outs, axis=-1)
    o_ref[...] = x_ref[...] + _dot(o, wo_ref[...])
    _rmsnorm_into(o_ref, gn_ref, hnext_ref)


def _cross_attention(x, g, wq, kv, wo, g_next, seq, tm):
    T, D = x.shape
    width = wq.shape[1]
    mem_len = kv.shape[1]
    tm = min(tm, seq)
    assert seq % tm == 0
    per_seq = seq // tm
    return pl.pallas_call(
        _xattn_kernel,
        grid=(T // tm,),
        in_specs=[
            pl.BlockSpec((tm, D), lambda i: (i, 0)),
            pl.BlockSpec((1, D), lambda i: (0, 0)),
            pl.BlockSpec((D, width), lambda i: (0, 0)),
            pl.BlockSpec((None, mem_len, 2 * width), lambda i: (i // per_seq, 0, 0)),
            pl.BlockSpec((width, D), lambda i: (0, 0)),
            pl.BlockSpec((1, D), lambda i: (0, 0)),
        ],
        out_specs=[pl.BlockSpec((tm, D), lambda i: (i, 0)), pl.BlockSpec((tm, D), lambda i: (i, 0))],
        out_shape=[jax.ShapeDtypeStruct((T, D), F32), jax.ShapeDtypeStruct((T, D), BF16)],
        scratch_shapes=[pltpu.VMEM((tm, D), BF16)],
        compiler_params=_params(1),
        name="memory_cross_attention",
    )(x, g.reshape(1, D), wq, kv, wo, g_next.reshape(1, D))


def _ffn_up_kernel(hn_ref, w1_ref, w3_ref, o_ref, w1b_ref, w3b_ref):
    @pl.when(pl.program_id(1) == 0)
    def _():
        rows = w1_ref.shape[0]
        chunk = min(256, rows)
        for r0 in range(0, rows, chunk):
            w1b_ref[r0:r0 + chunk, :] = w1_ref[r0:r0 + chunk, :].astype(w1b_ref.dtype)
            w3b_ref[r0:r0 + chunk, :] = w3_ref[r0:r0 + chunk, :].astype(w3b_ref.dtype)

    hn = hn_ref[...]
    a1 = _dot(hn, w1b_ref[...])
    a3 = _dot(hn, w3b_ref[...])
    o_ref[...] = (a1 * jax.nn.sigmoid(a1) * a3).astype(o_ref.dtype)


def _ffn_up(hn, w13, layer, tm, tn):
    T, D = hn.shape
    hidden = w13.shape[-1] // 2
    tm = min(tm, T)
    assert T % tm == 0 and hidden % tn == 0
    nb = hidden // tn
    return pl.pallas_call(
        _ffn_up_kernel,
        grid=(nb, T // tm),
        in_specs=[
            pl.BlockSpec((tm, D), lambda j, i: (i, 0)),
            pl.BlockSpec((None, D, tn), lambda j, i: (layer, 0, j)),
            pl.BlockSpec((None, D, tn), lambda j, i: (layer, 0, nb + j)),
        ],
        out_specs=pl.BlockSpec((tm, tn), lambda j, i: (i, j)),
        out_shape=jax.ShapeDtypeStruct((T, hidden), BF16),
        scratch_shapes=[pltpu.VMEM((D, tn), BF16), pltpu.VMEM((D, tn), BF16)],
        compiler_params=_params(2),
        name="ffn_up",
    )(hn, w13, w13)


def _rmsnorm_kernel(x_ref, g_ref, o_ref):
    _rmsnorm_into(x_ref, g_ref, o_ref)


def _rmsnorm(x, g, tm, out_dtype=F32):
    T, D = x.shape
    tm = min(tm, T)
    return pl.pallas_call(
        _rmsnorm_kernel,
        grid=(T // tm,),
        in_specs=[pl.BlockSpec((tm, D), lambda i: (i, 0)), pl.BlockSpec((1, D), lambda i: (0, 0))],
        out_specs=pl.BlockSpec((tm, D), lambda i: (i, 0)),
        out_shape=jax.ShapeDtypeStruct((T, D), out_dtype),
        compiler_params=_params(1),
        name="rmsnorm",
    )(x, g.reshape(1, D))


def kernel(x, mem, norm_mix, w_in, gate_bias, attn_sink, lru_conv_w, lru_conv_b, lru_wr, lru_br, lru_wi, lru_bi, lru_lambda, conv_dw_w, conv_dw_b, conv_ln_g, conv_ln_b, w_proj_attn, w_proj_lru, w_proj_conv, w_out, norm_cross, norm_mem, xattn_wq, xattn_wkv, xattn_wo, norm_ffn, ffn_w13, ffn_w2, norm_final):
    B, S, D = x.shape
    M = mem.shape[1]
    T = B * S
    depth = w_in.shape[0]
    attn_w = N_HEADS * HEAD_DIM
    qkv_w = attn_w + 2 * N_KV_HEADS * HEAD_DIM
    lru_w = lru_conv_w.shape[-1]
    conv_c = conv_dw_w.shape[-1]
    rest_w = w_in.shape[2] - qkv_w
    assert lru_w == D and conv_c == D
    u_col = 2 * lru_w
    gate_col = u_col + 2 * conv_c
    slopes = jnp.exp2(-(8.0 / N_HEADS) * jnp.arange(1, N_HEADS + 1, dtype=F32))

    xf = x.reshape(T, D)
    memf = mem.reshape(B * M, D)
    bf = lambda a: a.astype(BF16)
    for l in range(depth):
        hn = _rmsnorm(xf, norm_mix[l], tm=512, out_dtype=BF16)
        qkv = _proj_weight_resident(hn, w_in, l, 0, qkv_w, BF16, tm=1024, tn=1024)
        rest = _proj_weight_resident(hn, w_in, l, qkv_w, rest_w, F32, tm=1024, tn=1024)
        ya = _window_attention(qkv.reshape(B, S, qkv_w), slopes, attn_sink[l], B, S, tq=1024)
        yl = _rglru(rest.reshape(B, S, rest_w), lru_conv_w[l], lru_conv_b[l], bf(lru_wr[l]), lru_br[l],
                    bf(lru_wi[l]), lru_bi[l], lru_lambda[l], B, S, rc=128)
        merged = _conformer_merge(ya.reshape(T, attn_w), yl.reshape(T, lru_w), bf(w_proj_attn[l]), bf(w_proj_lru[l]),
                                  bf(w_proj_conv[l]), rest, u_col, gate_col, gate_bias[l].reshape(1, N_BRANCH * D),
                                  conv_dw_w[l], conv_dw_b[l], conv_ln_g[l], conv_ln_b[l], S, tm=512, tn=512,
                                  out_lead_rows=1024)
        xf = _residual_matmul(merged, bf(w_out[l]), None, xf, tm=1024, tn=1024, h_lead_rows=1024)
        kv = _norm_proj(memf, norm_mem[l], xattn_wkv, l, 0, xattn_wkv.shape[2], BF16, tm=1024, tn=512)
        xf, hn_ffn = _cross_attention(xf, norm_cross[l], bf(xattn_wq[l]), kv.reshape(B, M, -1), bf(xattn_wo[l]),
                                      norm_ffn[l], S, tm=512)
        hidden = _ffn_up(hn_ffn, ffn_w13, l, tm=1024, tn=512)
        xf = _residual_matmul(hidden, ffn_w2, l, xf, tm=1024, tn=256)
    return _rmsnorm(xf, norm_final, tm=512).reshape(B, S, D)
```

```python
import functools

import jax
import jax.numpy as jnp
from jax import lax
from jax.experimental import pallas as pl
from jax.experimental.pallas import tpu as pltpu

N_HEADS = 16
N_KV_HEADS = 4
HEAD_DIM = 128
WINDOW = 128
LRU_BLOCK_W = 128
LRU_CONV = 4
LRU_C = 8.0
XATTN_HEADS = 4
N_BRANCH = 3
EPS = 1e-6
NEG_INF = -1e30

SUBLANES = 8
LRU_SEG = 4
VMEM_LIMIT_BYTES = 56 * 1024 * 1024

BF16 = jnp.bfloat16
F32 = jnp.float32


def _params(n_grid_axes):
    return pltpu.CompilerParams(
        dimension_semantics=("arbitrary",) * n_grid_axes,
        vmem_limit_bytes=VMEM_LIMIT_BYTES,
    )


def _dot(a, b):
    return jnp.dot(a, b, preferred_element_type=F32)


def _dot_t(a, b):
    return lax.dot_general(a, b, (((1,), (1,)), ((), ())), preferred_element_type=F32)


def _rmsnorm_rows(x, g):
    y = x * lax.rsqrt(jnp.mean(x * x, axis=-1, keepdims=True) + EPS)
    return y * g


def _rmsnorm_into(x_ref, g_ref, out_ref, chunk=256):
    rows = x_ref.shape[0]
    chunk = min(chunk, rows)
    for r0 in range(0, rows, chunk):
        out_ref[r0:r0 + chunk, :] = _rmsnorm_rows(x_ref[r0:r0 + chunk, :], g_ref[...]).astype(out_ref.dtype)


def _proj_kernel(x_ref, g_ref, w_ref, o_ref, hn_ref):
    @pl.when(pl.program_id(1) == 0)
    def _():
        _rmsnorm_into(x_ref, g_ref, hn_ref)

    o_ref[...] = _dot(hn_ref[...], w_ref[...].astype(BF16)).astype(o_ref.dtype)


def _weight_spec(w, layer, tn, col_block_of):
    if w.ndim == 2:
        return pl.BlockSpec((w.shape[0], tn), lambda i, j: (0, col_block_of(j)))
    return pl.BlockSpec((None, w.shape[1], tn), lambda i, j: (layer, 0, col_block_of(j)))


def _norm_proj(x, g, w, layer, col_off, n_cols, out_dtype, tm, tn):
    T, K = x.shape
    tm = min(tm, T)
    assert T % tm == 0 and n_cols % tn == 0 and col_off % tn == 0
    off = col_off // tn
    return pl.pallas_call(
        _proj_kernel,
        grid=(T // tm, n_cols // tn),
        in_specs=[
            pl.BlockSpec((tm, K), lambda i, j: (i, 0)),
            pl.BlockSpec((1, K), lambda i, j: (0, 0)),
            _weight_spec(w, layer, tn, lambda j: j + off),
        ],
        out_specs=pl.BlockSpec((tm, tn), lambda i, j: (i, j)),
        out_shape=jax.ShapeDtypeStruct((T, n_cols), out_dtype),
        scratch_shapes=[pltpu.VMEM((tm, K), BF16)],
        compiler_params=_params(2),
        name="norm_proj",
    )(x, g.reshape(1, K), w)


def _proj_wres_kernel(hn_ref, w_ref, o_ref, wb_ref):
    @pl.when(pl.program_id(1) == 0)
    def _():
        rows = w_ref.shape[0]
        chunk = min(256, rows)
        for r0 in range(0, rows, chunk):
            wb_ref[r0:r0 + chunk, :] = w_ref[r0:r0 + chunk, :].astype(wb_ref.dtype)

    o_ref[...] = _dot(hn_ref[...], wb_ref[...]).astype(o_ref.dtype)


def _proj_weight_resident(hn, w, layer, col_off, n_cols, out_dtype, tm, tn):
    T, K = hn.shape
    tm = min(tm, T)
    assert T % tm == 0 and n_cols % tn == 0 and col_off % tn == 0
    off = col_off // tn
    return pl.pallas_call(
        _proj_wres_kernel,
        grid=(n_cols // tn, T // tm),
        in_specs=[
            pl.BlockSpec((tm, K), lambda j, i: (i, 0)),
            pl.BlockSpec((None, K, tn), lambda j, i: (layer, 0, j + off)),
        ],
        out_specs=pl.BlockSpec((tm, tn), lambda j, i: (i, j)),
        out_shape=jax.ShapeDtypeStruct((T, n_cols), out_dtype),
        scratch_shapes=[pltpu.VMEM((K, tn), BF16)],
        compiler_params=_params(2),
        name="proj_weight_resident",
    )(hn, w)


def _attn_kernel(slopes_ref, sink_ref, q_ref, k_ref, v_ref, o_ref, *, tq, seq):
    kh = pl.program_id(1)
    win = 3 * WINDOW
    scale = HEAD_DIM ** -0.5
    grp = N_HEADS // N_KV_HEADS
    rel = (lax.broadcasted_iota(jnp.int32, (WINDOW, win), 0)
           - lax.broadcasted_iota(jnp.int32, (WINDOW, win), 1))
    for j in range(tq // WINDOW):
        rows = slice(j * WINDOW, (j + 1) * WINDOW)
        qs = pl.program_id(2) * tq + j * WINDOW
        ws = pl.multiple_of(jnp.clip(qs - WINDOW, 0, seq - win), WINDOW)
        kb = k_ref[pl.ds(ws, win), :]
        vb = v_ref[pl.ds(ws, win), :]
        dist = jnp.abs(rel + (qs - ws))
        valid = dist <= WINDOW
        distf = dist.astype(F32)
        q_stack = jnp.concatenate(
            [q_ref[rows, g * HEAD_DIM:(g + 1) * HEAD_DIM] for g in range(grp)], axis=0)
        s_all = _dot_t(q_stack, kb) * scale
        probs, inv_den = [], []
        for g in range(grp):
            h = kh * grp + g
            s = s_all[g * WINDOW:(g + 1) * WINDOW]
            s = jnp.where(valid, s - slopes_ref[h] * distf, NEG_INF)
            sink = sink_ref[h]
            m = jnp.maximum(jnp.max(s, axis=-1, keepdims=True), sink)
            p = jnp.exp(s - m)
            inv_den.append(1.0 / (jnp.sum(p, axis=-1, keepdims=True) + jnp.exp(sink - m)))
            probs.append(p.astype(BF16))
        o_all = _dot(jnp.concatenate(probs, axis=0), vb)
        for g in range(grp):
            o = o_all[g * WINDOW:(g + 1) * WINDOW] * inv_den[g]
            o_ref[rows, g * HEAD_DIM:(g + 1) * HEAD_DIM] = o.astype(o_ref.dtype)


def _window_attention(qkv, slopes, sink, batch, seq, tq):
    tq = min(tq, seq)
    assert seq % tq == 0 and tq % WINDOW == 0 and seq >= 3 * WINDOW
    grp_w = (N_HEADS // N_KV_HEADS) * HEAD_DIM
    k_blk = N_HEADS
    v_blk = N_HEADS + N_KV_HEADS
    smem = pl.BlockSpec(memory_space=pltpu.SMEM)
    return pl.pallas_call(
        functools.partial(_attn_kernel, tq=tq, seq=seq),
        grid=(batch, N_KV_HEADS, seq // tq),
        in_specs=[
            smem,
            smem,
            pl.BlockSpec((None, tq, grp_w), lambda b, h, i: (b, i, h)),
            pl.BlockSpec((None, seq, HEAD_DIM), lambda b, h, i: (b, 0, k_blk + h)),
            pl.BlockSpec((None, seq, HEAD_DIM), lambda b, h, i: (b, 0, v_blk + h)),
        ],
        out_specs=pl.BlockSpec((None, tq, grp_w), lambda b, h, i: (b, i, h)),
        out_shape=jax.ShapeDtypeStruct((batch, seq, N_HEADS * HEAD_DIM), BF16),
        compiler_params=_params(3),
        name="window_attention",
    )(slopes, sink, qkv, qkv, qkv)


def _softplus(x):
    return jnp.maximum(x, 0.0) + jnp.log1p(jnp.exp(-jnp.abs(x)))


def _lru_kernel(xl_ref, gl_ref, cw_ref, cb_ref, wr_ref, br_ref, wi_ref, bi_ref, lam_ref, o_ref,
                xp_ref, y_ref, *, seq, rc):
    pad = SUBLANES
    cb = LRU_BLOCK_W
    group = SUBLANES * LRU_SEG
    n_groups = rc // group
    n_chunks = seq // rc
    xp_ref[0:pad, :] = jnp.zeros((pad, cb), F32)
    xp_ref[pad + seq:pad + seq + pad, :] = jnp.zeros((pad, cb), F32)
    xp_ref[pad:pad + seq, :] = xl_ref[...]
    row = lax.broadcasted_iota(jnp.int32, (SUBLANES, cb), 0)

    def bcast(v):
        return jnp.broadcast_to(v, (SUBLANES, cb))

    def seg_tile(ref, start):
        return ref[pl.ds(start, SUBLANES, stride=LRU_SEG), :]

    def gates(d, t0):
        w = [bcast(cw_ref[d, k:k + 1, :]) for k in range(LRU_CONV)]
        bias = bcast(cb_ref[d:d + 1, :])
        lo = -(LRU_CONV - 1) if d == 0 else 0
        xcs = []
        for g in range(n_groups):
            base = t0 + g * group + pad
            x = {o: seg_tile(xp_ref, base + o) for o in range(lo, lo + LRU_SEG + LRU_CONV - 1)}
            for j in range(LRU_SEG):
                acc = x[lo + j] * w[0]
                for k in range(1, LRU_CONV):
                    acc = acc + x[lo + j + k] * w[k]
                xcs.append(acc + bias)
        xc = jnp.concatenate(xcs, axis=0)
        xcb = xc.astype(BF16)
        r = jax.nn.sigmoid(_dot(xcb, wr_ref[d, 0]) + br_ref[d:d + 1, :])
        ig = jax.nn.sigmoid(_dot(xcb, wi_ref[d, 0]) + bi_ref[d:d + 1, :])
        log_a = r * (-LRU_C * _softplus(-lam_ref[d:d + 1, :]))
        a = jnp.exp(log_a)
        gain = jnp.sqrt(-jnp.tanh(log_a) * (a * a + 1.0))
        return a, gain * (ig * xc)

    def tile_scan(a, b, reverse):
        for k in (1, 2, 4):
            if reverse:
                keep = row < SUBLANES - k
                sh = SUBLANES - k
            else:
                keep = row >= k
                sh = k
            a_sh = jnp.where(keep, pltpu.roll(a, sh, 0), 1.0)
            b_sh = jnp.where(keep, pltpu.roll(b, sh, 0), 0.0)
            b = a * b_sh + b
            a = a * a_sh
        return a, b

    def group_scan(a, b, g, carry, reverse):
        tile = lambda v, j: v[(g * LRU_SEG + j) * SUBLANES:(g * LRU_SEG + j + 1) * SUBLANES]
        order = list(reversed(range(LRU_SEG))) if reverse else list(range(LRU_SEG))
        hloc, prod = {}, {}
        prev = None
        for j in order:
            aj, bj = tile(a, j), tile(b, j)
            if prev is None:
                hloc[j], prod[j] = bj, aj
            else:
                hloc[j], prod[j] = aj * hloc[prev] + bj, aj * prod[prev]
            prev = j
        a_seg, b_seg = tile_scan(prod[prev], hloc[prev], reverse)
        end = a_seg * carry + b_seg
        if reverse:
            cin = jnp.where(row == SUBLANES - 1, carry, pltpu.roll(end, SUBLANES - 1, 0))
            carry = bcast(end[0:1, :])
        else:
            cin = jnp.where(row == 0, carry, pltpu.roll(end, 1, 0))
            carry = bcast(end[SUBLANES - 1:SUBLANES, :])
        return [hloc[j] + prod[j] * cin for j in range(LRU_SEG)], carry

    def fwd_chunk(c, carry):
        t0 = c * rc
        a, b = gates(0, t0)
        for g in range(n_groups):
            hs, carry = group_scan(a, b, g, carry, False)
            for j in range(LRU_SEG):
                y_ref[pl.ds(t0 + g * group + j, SUBLANES, stride=LRU_SEG), :] = hs[j]
        return carry

    unroll = min(8, n_chunks)
    assert n_chunks % unroll == 0
    lax.fori_loop(0, n_chunks, fwd_chunk, jnp.zeros((SUBLANES, cb), F32), unroll=unroll)

    def bwd_chunk(c, carry):
        t0 = (n_chunks - 1 - c) * rc
        a, b = gates(1, t0)
        for g in reversed(range(n_groups)):
            hs, carry = group_scan(a, b, g, carry, True)
            for j in range(LRU_SEG):
                start = t0 + g * group + j
                y = (seg_tile(y_ref, start) + hs[j]) * jax.nn.gelu(seg_tile(gl_ref, start))
                y_ref[pl.ds(start, SUBLANES, stride=LRU_SEG), :] = y
        return carry

    lax.fori_loop(0, n_chunks, bwd_chunk, jnp.zeros((SUBLANES, cb), F32), unroll=unroll)
    oc = min(256, seq)
    for r0 in range(0, seq, oc):
        o_ref[r0:r0 + oc, :] = y_ref[r0:r0 + oc, :].astype(o_ref.dtype)


def _rglru(rest, conv_w, conv_b, wr, br, wi, bi, lam, batch, seq, rc):
    C = conv_w.shape[-1]
    cb = LRU_BLOCK_W
    rc = min(rc, seq)
    assert C % cb == 0 and seq % rc == 0 and rc % (SUBLANES * LRU_SEG) == 0
    ncb = C // cb
    nblk = 1
    vec = pl.BlockSpec((2, cb), lambda b, c: (0, c))
    wspec = pl.BlockSpec((2, nblk, LRU_BLOCK_W, LRU_BLOCK_W), lambda b, c: (0, c, 0, 0))
    return pl.pallas_call(
        functools.partial(_lru_kernel, seq=seq, rc=rc),
        grid=(batch, ncb),
        in_specs=[
            pl.BlockSpec((None, seq, cb), lambda b, c: (b, 0, c)),
            pl.BlockSpec((None, seq, cb), lambda b, c: (b, 0, ncb + c)),
            pl.BlockSpec((2, LRU_CONV, cb), lambda b, c: (0, 0, c)),
            vec, wspec, vec, wspec, vec, vec,
        ],
        out_specs=pl.BlockSpec((None, seq, cb), lambda b, c: (b, 0, c)),
        out_shape=jax.ShapeDtypeStruct((batch, seq, C), BF16),
        scratch_shapes=[pltpu.VMEM((seq + 2 * SUBLANES, cb), F32), pltpu.VMEM((seq, cb), F32)],
        compiler_params=_params(2),
        name="rglru",
    )(rest, rest, conv_w, conv_b, wr, br, wi, bi, lam)


def _conv_quarter(u1_ref, u2_ref, u1p_ref, u2p_ref, u1n_ref, u2n_ref, w_ref, b_ref, glu_ref, sh_ref, out_ref,
                  *, rows, halo, first, last, ktaps, rr, cc, hrows):
    width = out_ref.shape[-1]
    glu_ref[halo:halo + rows, :] = u1_ref[...] * jax.nn.sigmoid(u2_ref[...])
    prev = u1p_ref[...] * jax.nn.sigmoid(u2p_ref[...])
    glu_ref[0:halo, :] = jnp.where(first, 0.0, prev)
    nxt = u1n_ref[...] * jax.nn.sigmoid(u2n_ref[...])
    glu_ref[halo + rows:halo + rows + halo, :] = jnp.where(last, 0.0, nxt)
    half = (ktaps - 1) // 2
    base = halo - half
    n_shift_rows = hrows + 2 * halo - SUBLANES
    sr = 56
    assert n_shift_rows % sr == 0 and rows % hrows == 0 and hrows % rr == 0
    for h0 in range(0, rows, hrows):
        for s in range(1, SUBLANES):
            for r0 in range(0, n_shift_rows, sr):
                for c0 in range(0, width, cc):
                    cs = slice(c0, c0 + cc)
                    blk = glu_ref[h0 + r0:h0 + r0 + sr + SUBLANES, cs]
                    sh_ref[s - 1, r0:r0 + sr, cs] = pltpu.roll(blk, sr + SUBLANES - s, 0)[0:sr]
        for r0 in range(0, hrows, rr):
            for c0 in range(0, width, cc):
                cs = slice(c0, c0 + cc)
                n_out = rr // SUBLANES
                accs = [None] * n_out
                for s in range(SUBLANES):
                    taps = [k for k in range(ktaps) if (base + k) % SUBLANES == s]
                    if not taps:
                        continue
                    t_lo = min((base + k) // SUBLANES for k in taps)
                    t_hi = max((base + k) // SUBLANES for k in taps) + n_out
                    if s == 0:
                        tiles = {i: glu_ref[h0 + r0 + i * SUBLANES:h0 + r0 + (i + 1) * SUBLANES, cs]
                                 for i in range(t_lo, t_hi)}
                    else:
                        tiles = {i: sh_ref[s - 1, r0 + i * SUBLANES:r0 + (i + 1) * SUBLANES, cs]
                                 for i in range(t_lo, t_hi)}
                    for k in taps:
                        wk = w_ref[k, :, cs]
                        a = (base + k) // SUBLANES
                        for t in range(n_out):
                            term = tiles[a + t] * wk
                            accs[t] = term if accs[t] is None else accs[t] + term
                for t in range(n_out):
                    lo = h0 + r0 + t * SUBLANES
                    out_ref[lo:lo + SUBLANES, cs] = accs[t] + b_ref[:, cs]


def _conf_merge_kernel(ya_ref, yl_ref, wa_ref, wl_ref, wc_ref, g0_ref, g1_ref, g2_ref, b0_ref, b1_ref, b2_ref,
                       u1_ref, u2_ref, u1p_ref, u2p_ref, u1n_ref, u2n_ref, dw_ref, db_ref, lg_ref, lb_ref,
                       o_ref, glu_ref, sh_ref, conv_ref, yc_ref, *, rows, halo, seq, n_tiles, ktaps, nq):
    i = pl.program_id(0)
    j = pl.program_id(1)

    @pl.when(jnp.logical_and(i == 0, j == 0))
    def _():
        yc_ref[...] = jnp.zeros(yc_ref.shape, yc_ref.dtype)

    @pl.when(jnp.logical_and(i > 0, j == 0))
    def _():
        lr = min(64, rows)
        for r0 in range(0, rows, lr):
            c = jnp.concatenate([conv_ref[q, r0:r0 + lr, :] for q in range(nq)], axis=-1)
            mu = jnp.mean(c, axis=-1, keepdims=True)
            d = c - mu
            var = jnp.mean(d * d, axis=-1, keepdims=True)
            y = d * lax.rsqrt(var + EPS) * lg_ref[...] + lb_ref[...]
            yc_ref[r0:r0 + lr, :] = (y * jax.nn.sigmoid(y)).astype(yc_ref.dtype)

    t0 = jnp.minimum(i, n_tiles - 1) * rows
    _conv_quarter(u1_ref, u2_ref, u1p_ref, u2p_ref, u1n_ref, u2n_ref, dw_ref, db_ref, glu_ref, sh_ref,
                  conv_ref.at[j], rows=rows, halo=halo, first=(t0 % seq) == 0, last=((t0 + rows) % seq) == 0,
                  ktaps=ktaps, rr=64, cc=256, hrows=min(256, rows))

    def branch(y_ref, w_ref, g_ref, b_ref):
        return jax.nn.sigmoid(g_ref[...] + b_ref[...]) * _dot(y_ref[...], w_ref[...])

    m = branch(ya_ref, wa_ref, g0_ref, b0_ref) + branch(yl_ref, wl_ref, g1_ref, b1_ref)
    m = m + branch(yc_ref, wc_ref, g2_ref, b2_ref)
    o_ref[...] = m.astype(o_ref.dtype)


def _conformer_merge(ya, yl, wa, wl, wc, rest, u_col, gate_col, gate_bias, dw_w, dw_b, ln_g, ln_b, seq, tm, tn,
                     out_lead_rows):
    T, D = ya.shape
    ktaps, C = dw_w.shape
    halo = 2 * SUBLANES
    tm = min(tm, seq)
    assert C == D and T % tm == 0 and seq % tm == 0 and D % tn == 0 and tm % halo == 0
    assert gate_col % tn == 0 and u_col % tn == 0 and (ktaps - 1) // 2 <= halo
    nq = D // tn
    n_tiles = T // tm
    gc, uc = gate_col // tn, u_col // tn
    hb = tm // halo
    n_halo_blocks = T // halo
    mt = lambda i: jnp.maximum(i - 1, 0)
    ct = lambda i: jnp.minimum(i, n_tiles - 1)
    act = pl.BlockSpec((tm, D), lambda i, j: (mt(i), 0))
    wsp = pl.BlockSpec((D, tn), lambda i, j: (0, j))
    gate = lambda n: pl.BlockSpec((tm, tn), lambda i, j: (mt(i), gc + n * nq + j))
    bias = lambda n: pl.BlockSpec((1, tn), lambda i, j: (0, n * nq + j))
    main = lambda n: pl.BlockSpec((tm, tn), lambda i, j: (ct(i), uc + n * nq + j))
    prev = lambda n: pl.BlockSpec((halo, tn), lambda i, j: (jnp.maximum(ct(i) * hb - 1, 0), uc + n * nq + j))
    nxt = lambda n: pl.BlockSpec(
        (halo, tn), lambda i, j: (jnp.minimum((ct(i) + 1) * hb, n_halo_blocks - 1), uc + n * nq + j))
    vec = pl.BlockSpec((1, C), lambda i, j: (0, 0))
    hrows = min(256, tm)
    lead = out_lead_rows // tm
    assert out_lead_rows % tm == 0 and lead >= 1
    return pl.pallas_call(
        functools.partial(_conf_merge_kernel, rows=tm, halo=halo, seq=seq, n_tiles=n_tiles, ktaps=ktaps, nq=nq),
        grid=(n_tiles + 1, nq),
        in_specs=[act, act, wsp, wsp, wsp, gate(0), gate(1), gate(2), bias(0), bias(1), bias(2),
                  main(0), main(1), prev(0), prev(1), nxt(0), nxt(1),
                  pl.BlockSpec((ktaps, SUBLANES, tn), lambda i, j: (0, 0, j)),
                  pl.BlockSpec((1, tn), lambda i, j: (0, j)), vec, vec],
        out_specs=pl.BlockSpec((tm, tn), lambda i, j: (i + lead - 1, j)),
        out_shape=jax.ShapeDtypeStruct((T + lead * tm, D), BF16),
        scratch_shapes=[pltpu.VMEM((tm + 2 * halo, tn), F32),
                        pltpu.VMEM((SUBLANES - 1, hrows + 2 * halo - SUBLANES, tn), F32),
                        pltpu.VMEM((nq, tm, tn), F32),
                        pltpu.VMEM((tm, D), BF16)],
        compiler_params=_params(2),
        name="conformer_merge",
    )(ya, yl, wa, wl, wc, rest, rest, rest, gate_bias, gate_bias, gate_bias,
      rest, rest, rest, rest, rest, rest, jnp.broadcast_to(dw_w[:, None, :], (ktaps, SUBLANES, C)),
      dw_b.reshape(1, C), ln_g.reshape(1, C), ln_b.reshape(1, C))


def _resmm_kernel(h_ref, w_ref, x_ref, o_ref):
    o_ref[...] = x_ref[...] + _dot(h_ref[...], w_ref[...].astype(BF16))


def _residual_matmul(h, w, layer, x, tm, tn, h_lead_rows=0):
    T, K = x.shape[0], h.shape[1]
    N = w.shape[-1]
    tm = min(tm, T)
    assert T % tm == 0 and N % tn == 0 and h_lead_rows % tm == 0
    lead = h_lead_rows // tm
    return pl.pallas_call(
        _resmm_kernel,
        grid=(T // tm, N // tn),
        in_specs=[
            pl.BlockSpec((tm, K), lambda i, j: (i + lead, 0)),
            _weight_spec(w, layer, tn, lambda j: j),
            pl.BlockSpec((tm, tn), lambda i, j: (i, j)),
        ],
        out_specs=pl.BlockSpec((tm, tn), lambda i, j: (i, j)),
        out_shape=jax.ShapeDtypeStruct((T, N), F32),
        compiler_params=_params(2),
        name="residual_matmul",
    )(h, w, x)


def _xattn_kernel(x_ref, g_ref, wq_ref, kv_ref, wo_ref, gn_ref, o_ref, hnext_ref, hn_ref):
    _rmsnorm_into(x_ref, g_ref, hn_ref)
    q = _dot(hn_ref[...], wq_ref[...]).astype(BF16)
    width = wq_ref.shape[1]
    hd = width // XATTN_HEADS
    scale = hd ** -0.5
    outs = []
    for h in range(XATTN_HEADS):
        cols = slice(h * hd, (h + 1) * hd)
        s = _dot_t(q[:, cols], kv_ref[:, cols]) * scale
        m = jnp.max(s, axis=-1, keepdims=True)
        p = jnp.exp(s - m)
        denom = jnp.sum(p, axis=-1, keepdims=True)
        vcols = slice(width + h * hd, width + (h + 1) * hd)
        outs.append((_dot(p.astype(BF16), kv_ref[:, vcols]) / denom).astype(BF16))
    o = jnp.concatenate(outs, axis=-1)
    o_ref[...] = x_ref[...] + _dot(o, wo_ref[...])
    _rmsnorm_into(o_ref, gn_ref, hnext_ref)


def _cross_attention(x, g, wq, kv, wo, g_next, seq, tm):
    T, D = x.shape
    width = wq.shape[1]
    mem_len = kv.shape[1]
    tm = min(tm, seq)
    assert seq % tm == 0
    per_seq = seq // tm
    return pl.pallas_call(
        _xattn_kernel,
        grid=(T // tm,),
        in_specs=[
            pl.BlockSpec((tm, D), lambda i: (i, 0)),
            pl.BlockSpec((1, D), lambda i: (0, 0)),
            pl.BlockSpec((D, width), lambda i: (0, 0)),
            pl.BlockSpec((None, mem_len, 2 * width), lambda i: (i // per_seq, 0, 0)),
            pl.BlockSpec((width, D), lambda i: (0, 0)),
            pl.BlockSpec((1, D), lambda i: (0, 0)),
        ],
        out_specs=[pl.BlockSpec((tm, D), lambda i: (i, 0)), pl.BlockSpec((tm, D), lambda i: (i, 0))],
        out_shape=[jax.ShapeDtypeStruct((T, D), F32), jax.ShapeDtypeStruct((T, D), BF16)],
        scratch_shapes=[pltpu.VMEM((tm, D), BF16)],
        compiler_params=_params(1),
        name="memory_cross_attention",
    )(x, g.reshape(1, D), wq, kv, wo, g_next.reshape(1, D))


def _ffn_up_kernel(hn_ref, w1_ref, w3_ref, o_ref, w1b_ref, w3b_ref):
    @pl.when(pl.program_id(1) == 0)
    def _():
        rows = w1_ref.shape[0]
        chunk = min(256, rows)
        for r0 in range(0, rows, chunk):
            w1b_ref[r0:r0 + chunk, :] = w1_ref[r0:r0 + chunk, :].astype(w1b_ref.dtype)
            w3b_ref[r0:r0 + chunk, :] = w3_ref[r0:r0 + chunk, :].astype(w3b_ref.dtype)

    hn = hn_ref[...]
    a1 = _dot(hn, w1b_ref[...])
    a3 = _dot(hn, w3b_ref[...])
    o_ref[...] = (a1 * jax.nn.sigmoid(a1) * a3).astype(o_ref.dtype)


def _ffn_up(hn, w13, layer, tm, tn):
    T, D = hn.shape
    hidden = w13.shape[-1] // 2
    tm = min(tm, T)
    assert T % tm == 0 and hidden % tn == 0
    nb = hidden // tn
    return pl.pallas_call(
        _ffn_up_kernel,
        grid=(nb, T // tm),
        in_specs=[
            pl.BlockSpec((tm, D), lambda j, i: (i, 0)),
            pl.BlockSpec((None, D, tn), lambda j, i: (layer, 0, j)),
            pl.BlockSpec((None, D, tn), lambda j, i: (layer, 0, nb + j)),
        ],
        out_specs=pl.BlockSpec((tm, tn), lambda j, i: (i, j)),
        out_shape=jax.ShapeDtypeStruct((T, hidden), BF16),
        scratch_shapes=[pltpu.VMEM((D, tn), BF16), pltpu.VMEM((D, tn), BF16)],
        compiler_params=_params(2),
        name="ffn_up",
    )(hn, w13, w13)


def _rmsnorm_kernel(x_ref, g_ref, o_ref):
    _rmsnorm_into(x_ref, g_ref, o_ref)


def _rmsnorm(x, g, tm, out_dtype=F32):
    T, D = x.shape
    tm = min(tm, T)
    return pl.pallas_call(
        _rmsnorm_kernel,
        grid=(T // tm,),
        in_specs=[pl.BlockSpec((tm, D), lambda i: (i, 0)), pl.BlockSpec((1, D), lambda i: (0, 0))],
        out_specs=pl.BlockSpec((tm, D), lambda i: (i, 0)),
        out_shape=jax.ShapeDtypeStruct((T, D), out_dtype),
        compiler_params=_params(1),
        name="rmsnorm",
    )(x, g.reshape(1, D))


def kernel(x, mem, norm_mix, w_in, gate_bias, attn_sink, lru_conv_w, lru_conv_b, lru_wr, lru_br, lru_wi, lru_bi, lru_lambda, conv_dw_w, conv_dw_b, conv_ln_g, conv_ln_b, w_proj_attn, w_proj_lru, w_proj_conv, w_out, norm_cross, norm_mem, xattn_wq, xattn_wkv, xattn_wo, norm_ffn, ffn_w13, ffn_w2, norm_final):
    B, S, D = x.shape
    M = mem.shape[1]
    T = B * S
    depth = w_in.shape[0]
    attn_w = N_HEADS * HEAD_DIM
    qkv_w = attn_w + 2 * N_KV_HEADS * HEAD_DIM
    lru_w = lru_conv_w.shape[-1]
    conv_c = conv_dw_w.shape[-1]
    rest_w = w_in.shape[2] - qkv_w
    assert lru_w == D and conv_c == D
    u_col = 2 * lru_w
    gate_col = u_col + 2 * conv_c
    slopes = jnp.exp2(-(8.0 / N_HEADS) * jnp.arange(1, N_HEADS + 1, dtype=F32))

    xf = x.reshape(T, D)
    memf = mem.reshape(B * M, D)
    bf = lambda a: a.astype(BF16)
    for l in range(depth):
        hn = _rmsnorm(xf, norm_mix[l], tm=512, out_dtype=BF16)
        qkv = _proj_weight_resident(hn, w_in, l, 0, qkv_w, BF16, tm=1024, tn=1024)
        rest = _proj_weight_resident(hn, w_in, l, qkv_w, rest_w, F32, tm=1024, tn=1024)
        ya = _window_attention(qkv.reshape(B, S, qkv_w), slopes, attn_sink[l], B, S, tq=1024)
        yl = _rglru(rest.reshape(B, S, rest_w), lru_conv_w[l], lru_conv_b[l], bf(lru_wr[l]), lru_br[l],
                    bf(lru_wi[l]), lru_bi[l], lru_lambda[l], B, S, rc=256)
        merged = _conformer_merge(ya.reshape(T, attn_w), yl.reshape(T, lru_w), bf(w_proj_attn[l]), bf(w_proj_lru[l]),
                                  bf(w_proj_conv[l]), rest, u_col, gate_col, gate_bias[l].reshape(1, N_BRANCH * D),
                                  conv_dw_w[l], conv_dw_b[l], conv_ln_g[l], conv_ln_b[l], S, tm=512, tn=512,
                                  out_lead_rows=1024)
        xf = _residual_matmul(merged, bf(w_out[l]), None, xf, tm=1024, tn=1024, h_lead_rows=1024)
        kv = _norm_proj(memf, norm_mem[l], xattn_wkv, l, 0, xattn_wkv.shape[2], BF16, tm=1024, tn=512)
        xf, hn_ffn = _cross_attention(xf, norm_cross[l], bf(xattn_wq[l]), kv.reshape(B, M, -1), bf(xattn_wo[l]),
                                      norm_ffn[l], S, tm=512)
        hidden = _ffn_up(hn_ffn, ffn_w13, l, tm=1024, tn=512)
        xf = _residual_matmul(hidden, ffn_w2, l, xf, tm=1024, tn=256)
    return _rmsnorm(xf, norm_final, tm=512).reshape(B, S, D)
```
